```python
import math
import jax, jax.numpy as jnp
from jax import lax
import numpy as np

D_MODEL = 2048
BATCH = 4
SEQ = 4096
DEPTH = 4

CHUNK = 64
Q_BLOCK = 128
HEAD_DIM = 128
FOX_HEADS = 8
FOX_WIDTH = FOX_HEADS * HEAD_DIM
SGU_GROUPS = 8
SGU_WIDTH = SGU_GROUPS * HEAD_DIM
SGU_GROUP_DIM = SGU_WIDTH // SGU_GROUPS
SGU_SPAN = 128
GDN_HEADS = 8
GDN_WIDTH = GDN_HEADS * HEAD_DIM
GDN_CONV = 4
N_BRANCH = 3
D_FF = 5504
FFN_CONV = 3
DEEPNORM_ALPHA = (2 * DEPTH) ** 0.25
DEEPNORM_BETA = (8 * DEPTH) ** -0.25
LN_EPS = 1e-5
RMS_EPS = 1e-6

IN_SIZES = (3 * FOX_WIDTH,
            FOX_HEADS,
            2 * SGU_WIDTH,
            3 * GDN_WIDTH,
            GDN_HEADS,
            GDN_HEADS,
            GDN_WIDTH,
            N_BRANCH * D_MODEL)
IN_OFFSETS = tuple(sum(IN_SIZES[:i]) for i in range(1, len(IN_SIZES)))
N_IN = sum(IN_SIZES)
FOX_F_OFFSET = 3 * FOX_WIDTH

kernel_name = "hybrid_fox_sgu_gdn_convffn_block"


def layer_norm(x, g, b):
    xf = x.astype(jnp.float32)
    mu = jnp.mean(xf, axis=-1, keepdims=True)
    var = jnp.mean(jnp.square(xf - mu), axis=-1, keepdims=True)
    return ((xf - mu) * lax.rsqrt(var + LN_EPS) * g + b).astype(x.dtype)


def causal_depthwise_conv(x, w):
    k = w.shape[0]
    c = x.shape[-1]
    return lax.conv_general_dilated(
        x, w[:, None, :].astype(x.dtype), window_strides=(1,),
        padding=[(k - 1, 0)], dimension_numbers=('NWC', 'WIO', 'NWC'),
        feature_group_count=c)


def forgetting_attention(q, k, v, log_f):
    b, s, h, dh = q.shape
    nb = s // Q_BLOCK
    c = jnp.cumsum(log_f, axis=1).transpose(0, 2, 1)
    qb = q.reshape(b, nb, Q_BLOCK, h, dh).transpose(1, 0, 3, 2, 4)
    cqb = c.reshape(b, h, nb, Q_BLOCK).transpose(2, 0, 1, 3)
    kpos = jnp.arange(s)
    scale = dh ** -0.5

    def block(args):
        qi, cqi, i = args
        logits = jnp.einsum('bhqd,bkhd->bhqk', qi, k).astype(jnp.float32) * scale
        logits = logits + cqi[..., None] - c[:, :, None, :]
        qpos = i * Q_BLOCK + jnp.arange(Q_BLOCK)
        mask = qpos[:, None] >= kpos[None, :]
        p = jax.nn.softmax(jnp.where(mask, logits, -jnp.inf), axis=-1)
        return jnp.einsum('bhqk,bkhd->bqhd', p.astype(v.dtype), v)

    out = lax.map(block, (qb, cqb, jnp.arange(nb)))
    return out.transpose(1, 0, 2, 3, 4).reshape(b, s, h * dh)


def spatial_gating(u, v, ln_g, ln_b, w_s, b_s):
    b, s, w = v.shape
    n = s // SGU_SPAN
    shape5 = (b, n, SGU_SPAN, SGU_GROUPS, SGU_GROUP_DIM)
    vg = layer_norm(v.reshape(shape5), ln_g.reshape(SGU_GROUPS, SGU_GROUP_DIM),
                    ln_b.reshape(SGU_GROUPS, SGU_GROUP_DIM))
    pos = jnp.arange(SGU_SPAN) // CHUNK
    mask = pos[:, None] >= pos[None, :]
    mixed = jnp.einsum('gts,bnsgc->bntgc', jnp.where(mask, w_s, 0.0).astype(vg.dtype), vg)
    mixed = mixed + b_s.T[:, :, None]
    return (u.reshape(shape5) * mixed).reshape(b, s, w)


def l2_normalize(t):
    return t * lax.rsqrt(jnp.sum(jnp.square(t), axis=-1, keepdims=True) + RMS_EPS)


def gated_delta_rule(q, k, v, g, beta):
    b, s, h, dh = q.shape
    n = s // CHUNK
    to_c = lambda t: t.transpose(0, 2, 1, 3).reshape(b, h, n, CHUNK, t.shape[-1])
    q, k, v = to_c(q) * dh ** -0.5, to_c(k), to_c(v)
    g = g.transpose(0, 2, 1).reshape(b, h, n, CHUNK)
    beta = beta.transpose(0, 2, 1).reshape(b, h, n, CHUNK)
    gc = jnp.cumsum(g, axis=-1)
    causal = jnp.tril(jnp.ones((CHUNK, CHUNK), bool))
    strict = jnp.tril(jnp.ones((CHUNK, CHUNK), bool), -1)
    decay = jnp.exp(jnp.where(causal, gc[..., :, None] - gc[..., None, :], -jnp.inf))
    k_beta = k * beta[..., None]
    a_kk = jnp.where(strict, jnp.einsum('bhnid,bhnjd->bhnij', k_beta, k) * decay, 0.0)
    eye = jnp.eye(CHUNK, dtype=q.dtype)
    rhs = jnp.concatenate([v * beta[..., None], k_beta * jnp.exp(gc)[..., None]], axis=-1)
    sol = lax.linalg.triangular_solve(eye + a_kk, rhs, left_side=True, lower=True)
    u, w = sol[..., :dh], sol[..., dh:]
    qk = jnp.where(causal, jnp.einsum('bhnid,bhnjd->bhnij', q, k) * decay, 0.0)
    g_last = gc[..., -1]
    k_dec = k * jnp.exp(g_last[..., None] - gc)[..., None]
    q_dec = q * jnp.exp(gc)[..., None]

    def step(state, xs):
        qd, kd, qk_i, u_i, w_i, gl = xs
        v_new = u_i - jnp.einsum('bhcd,bhde->bhce', w_i, state)
        o = jnp.einsum('bhcd,bhde->bhce', qd, state) + jnp.einsum('bhij,bhje->bhie', qk_i, v_new)
        state = state * jnp.exp(gl)[..., None, None] + jnp.einsum('bhcd,bhce->bhde', kd, v_new)
        return state, o

    xs = tuple(jnp.moveaxis(t, 2, 0) for t in (q_dec, k_dec, qk, u, w, g_last))
    state0 = jnp.zeros((b, h, dh, dh), jnp.float32)
    _, o = lax.scan(step, state0, xs)
    return jnp.moveaxis(o, 0, 2).reshape(b, h, s, dh).transpose(0, 2, 1, 3)


def gdn_mixer(qkv, a, beta_logit, gate, conv_w, a_log, dt_bias, norm_g):
    b, s, _ = qkv.shape
    qkv = jax.nn.silu(causal_depthwise_conv(qkv, conv_w)).astype(jnp.float32)
    q, k, v = [t.reshape(b, s, GDN_HEADS, HEAD_DIM) for t in jnp.split(qkv, 3, axis=-1)]
    q, k = l2_normalize(q), l2_normalize(k)
    g = -jnp.exp(a_log.astype(jnp.float32)) * jax.nn.softplus((a + dt_bias).astype(jnp.float32))
    beta = jax.nn.sigmoid(beta_logit.astype(jnp.float32))
    o = gated_delta_rule(q, k, v, g, beta)
    o = o * lax.rsqrt(jnp.mean(jnp.square(o), axis=-1, keepdims=True) + RMS_EPS) * norm_g
    o = o * jax.nn.silu(gate.reshape(b, s, GDN_HEADS, HEAD_DIM).astype(jnp.float32))
    return o.astype(gate.dtype).reshape(b, s, GDN_WIDTH)


def token_mixing(x, w_in, b_in, sgu_ln_g, sgu_ln_b, sgu_w, sgu_b, gdn_conv_w,
                 gdn_a_log, gdn_dt_bias, gdn_norm_g, w_proj_a, w_proj_b, w_proj_c, w_out):
    bsz, s, _ = x.shape
    proj = x @ w_in + b_in
    fox_qkv, fox_f, sgu_uv, gdn_qkv, gdn_a, gdn_b, gdn_gate, gates = jnp.split(
        proj, list(IN_OFFSETS), axis=-1)
    fq, fk, fv = [t.reshape(bsz, s, FOX_HEADS, HEAD_DIM) for t in jnp.split(fox_qkv, 3, axis=-1)]
    y_a = forgetting_attention(fq, fk, fv, jax.nn.log_sigmoid(fox_f.astype(jnp.float32)))
    su, sv = jnp.split(sgu_uv, 2, axis=-1)
    y_b = spatial_gating(su, sv, sgu_ln_g, sgu_ln_b, sgu_w, sgu_b)
    y_c = gdn_mixer(gdn_qkv, gdn_a, gdn_b, gdn_gate, gdn_conv_w, gdn_a_log, gdn_dt_bias, gdn_norm_g)
    gt = jax.nn.sigmoid(gates).reshape(bsz, s, N_BRANCH, D_MODEL)
    merged = (gt[:, :, 0] * (y_a @ w_proj_a) + gt[:, :, 1] * (y_b @ w_proj_b)
              + gt[:, :, 2] * (y_c @ w_proj_c))
    return merged @ w_out


def conv_ffn(x, w_up, conv_w, conv_b, w_down):
    h = causal_depthwise_conv(x @ w_up, conv_w) + conv_b
    h_gate, h_val = jnp.split(h, 2, axis=-1)
    return (jax.nn.silu(h_gate) * h_val) @ w_down


def setup_inputs(seed: int = 0) -> dict:
    key = jax.random.key(seed)
    ks = jax.random.split(key, 24)
    L, D = DEPTH, D_MODEL
    f32 = jnp.float32
    nrm = lambda k, shape, scale: jax.random.normal(k, shape, f32) * scale
    x = nrm(ks[0], (BATCH, SEQ, D), 1.0)
    w_in = nrm(ks[1], (L, D, N_IN), D ** -0.5)
    b_in = nrm(ks[2], (L, N_IN), 0.01)
    fox_fb = jax.random.uniform(ks[3], (L, FOX_HEADS), f32, 1.0, 5.0)
    b_in = b_in.at[:, FOX_F_OFFSET:FOX_F_OFFSET + FOX_HEADS].add(fox_fb)
    sgu_ln_g = 1.0 + nrm(ks[4], (L, SGU_WIDTH), 0.01)
    sgu_ln_b = nrm(ks[5], (L, SGU_WIDTH), 0.01)
    sgu_w = nrm(ks[6], (L, SGU_GROUPS, SGU_SPAN, SGU_SPAN), SGU_SPAN ** -0.5)
    sgu_b = 1.0 + nrm(ks[7], (L, SGU_GROUPS, SGU_SPAN), 0.01)
    gdn_conv_w = nrm(ks[8], (L, GDN_CONV, 3 * GDN_WIDTH), GDN_CONV ** -0.5)
    gdn_a_log = jnp.log(jax.random.uniform(ks[9], (L, GDN_HEADS), f32, 1.0, 16.0))
    dt = jnp.exp(jax.random.uniform(ks[10], (L, GDN_HEADS), f32, math.log(1e-3), math.log(1e-1)))
    gdn_dt_bias = dt + jnp.log(-jnp.expm1(-dt))
    gdn_norm_g = 1.0 + nrm(ks[11], (L, HEAD_DIM), 0.01)
    w_proj_a = nrm(ks[12], (L, FOX_WIDTH, D), FOX_WIDTH ** -0.5 * DEEPNORM_BETA)
    w_proj_b = nrm(ks[13], (L, SGU_WIDTH, D), SGU_WIDTH ** -0.5 * DEEPNORM_BETA)
    w_proj_c = nrm(ks[14], (L, GDN_WIDTH, D), GDN_WIDTH ** -0.5 * DEEPNORM_BETA)
    w_out = nrm(ks[15], (L, D, D), D ** -0.5 * DEEPNORM_BETA)
    ln1_g = 1.0 + nrm(ks[16], (L, D), 0.01)
    ln1_b = nrm(ks[17], (L, D), 0.01)
    ffn_w_up = nrm(ks[18], (L, D, 2 * D_FF), D ** -0.5)
    ffn_conv_w = nrm(ks[19], (L, FFN_CONV, 2 * D_FF), FFN_CONV ** -0.5)
    ffn_conv_b = nrm(ks[20], (L, 2 * D_FF), 0.01)
    ffn_w_down = nrm(ks[21], (L, D_FF, D), D_FF ** -0.5 * DEEPNORM_BETA)
    ln2_g = 1.0 + nrm(ks[22], (L, D), 0.01)
    ln2_b = nrm(ks[23], (L, D), 0.01)
    return {"x": x, "w_in": w_in, "b_in": b_in, "sgu_ln_g": sgu_ln_g, "sgu_ln_b": sgu_ln_b,
            "sgu_w": sgu_w, "sgu_b": sgu_b, "gdn_conv_w": gdn_conv_w, "gdn_a_log": gdn_a_log,
            "gdn_dt_bias": gdn_dt_bias, "gdn_norm_g": gdn_norm_g, "w_proj_a": w_proj_a,
            "w_proj_b": w_proj_b, "w_proj_c": w_proj_c, "w_out": w_out, "ln1_g": ln1_g,
            "ln1_b": ln1_b, "ffn_w_up": ffn_w_up, "ffn_conv_w": ffn_conv_w,
            "ffn_conv_b": ffn_conv_b, "ffn_w_down": ffn_w_down, "ln2_g": ln2_g, "ln2_b": ln2_b}


def reference(x, w_in, b_in, sgu_ln_g, sgu_ln_b, sgu_w, sgu_b, gdn_conv_w, gdn_a_log,
              gdn_dt_bias, gdn_norm_g, w_proj_a, w_proj_b, w_proj_c, w_out, ln1_g, ln1_b,
              ffn_w_up, ffn_conv_w, ffn_conv_b, ffn_w_down, ln2_g, ln2_b):
    for l in range(DEPTH):
        mix = token_mixing(x, w_in[l], b_in[l], sgu_ln_g[l], sgu_ln_b[l], sgu_w[l], sgu_b[l],
                           gdn_conv_w[l], gdn_a_log[l], gdn_dt_bias[l], gdn_norm_g[l],
                           w_proj_a[l], w_proj_b[l], w_proj_c[l], w_out[l])
        x = layer_norm(DEEPNORM_ALPHA * x + mix, ln1_g[l], ln1_b[l])
        ffn = conv_ffn(x, ffn_w_up[l], ffn_conv_w[l], ffn_conv_b[l], ffn_w_down[l])
        x = layer_norm(DEEPNORM_ALPHA * x + ffn, ln2_g[l], ln2_b[l])
    return x
```

```python
import functools

import jax
import jax.numpy as jnp
from jax import lax
from jax.experimental import pallas as pl
from jax.experimental.pallas import tpu as pltpu

F32 = jnp.float32
BF16 = jnp.bfloat16

D_MODEL = 2048
HEAD_DIM = 128
N_HEADS = 8
WIDTH = N_HEADS * HEAD_DIM
CHUNK = 64
SGU_SPAN = 128
GDN_CONV = 4
D_FF = 5504
D_FF_PAD = 5632
FFN_CONV = 3
DEPTH = 4
DEEPNORM_ALPHA = (2 * DEPTH) ** 0.25
LN_EPS = 1e-5
RMS_EPS = 1e-6

AUX_LANES = 128
AUX_C = 0
AUX_GC = 8
AUX_BETA = 16

VMEM_LIMIT = 56 * 1024 * 1024

NEG_INF = float("-inf")


def _params(*semantics):
    return pltpu.CompilerParams(dimension_semantics=semantics, vmem_limit_bytes=VMEM_LIMIT)


def _dot(a, b):
    return jnp.dot(a, b, preferred_element_type=F32)


def _dot_nt(a, b):
    return lax.dot_general(a, b, (((1,), (1,)), ((), ())), preferred_element_type=F32)


def _dot_tn(a, b):
    return lax.dot_general(a, b, (((0,), (0,)), ((), ())), preferred_element_type=F32)


def _split2(x):
    hi = x.astype(BF16)
    lo = (x - hi.astype(F32)).astype(BF16)
    return hi, lo


def _split3(x):
    hi = x.astype(BF16)
    r1 = x - hi.astype(F32)
    mid = r1.astype(BF16)
    lo = (r1 - mid.astype(F32)).astype(BF16)
    return hi, mid, lo


def _mm3(a, b):
    ah, al = _split2(a)
    bh, bl = _split2(b)
    return _dot(ah, bh) + _dot(ah, bl) + _dot(al, bh)


def _softplus(z):
    return jnp.maximum(z, 0.0) + jnp.log1p(jnp.exp(-jnp.abs(z)))


def _sigmoid(z):
    return 1.0 / (1.0 + jnp.exp(-z))


def _layer_norm_rows(y, g, b):
    mu = jnp.mean(y, axis=-1, keepdims=True)
    yc = y - mu
    var = jnp.mean(yc * yc, axis=-1, keepdims=True)
    return yc * lax.rsqrt(var + LN_EPS) * g + b


def _mm_bias_kernel(x_ref, w_ref, b_ref, o_ref):
    acc = _dot(x_ref[...], w_ref[...])
    o_ref[...] = (acc + b_ref[...]).astype(o_ref.dtype)


def matmul_bias(xb, w, b, out_dtype, tm, tn, name):
    m, k = xb.shape
    n = w.shape[1]
    return pl.pallas_call(
        _mm_bias_kernel,
        grid=(m // tm, n // tn),
        in_specs=[pl.BlockSpec((tm, k), lambda i, j: (i, 0)),
                  pl.BlockSpec((k, tn), lambda i, j: (0, j)),
                  pl.BlockSpec((1, tn), lambda i, j: (0, j))],
        out_specs=pl.BlockSpec((tm, tn), lambda i, j: (i, j)),
        out_shape=jax.ShapeDtypeStruct((m, n), out_dtype),
        compiler_params=_params("parallel", "arbitrary"),
        name=name,
    )(xb, w, b)


AUX_BLOCK = 256


def _aux_kernel(z_ref, alog_ref, dt_ref, o_ref, carry_ref):
    @pl.when(pl.program_id(1) == 0)
    def _():
        carry_ref[...] = jnp.zeros_like(carry_ref)

    z = z_ref[0]
    lane = lax.broadcasted_iota(jnp.int32, z.shape, 1)
    log_f = -_softplus(-z)
    g = -jnp.exp(alog_ref[...]) * _softplus(z + dt_ref[...])
    beta = _sigmoid(z)

    ri = lax.broadcasted_iota(jnp.int32, (AUX_BLOCK, AUX_BLOCK), 0)
    ci = lax.broadcasted_iota(jnp.int32, (AUX_BLOCK, AUX_BLOCK), 1)
    tri = ri >= ci
    ones_seq = jnp.where(tri, 1.0, 0.0).astype(BF16)
    ones_chunk = jnp.where(tri & ((ri // CHUNK) == (ci // CHUNK)), 1.0, 0.0).astype(BF16)

    def cumsum(ones, val):
        hi, mid, lo = _split3(val)
        return _dot(ones, hi) + _dot(ones, mid) + _dot(ones, lo)

    c = cumsum(ones_seq, log_f) + carry_ref[...]
    carry_ref[...] = c[AUX_BLOCK - 1:AUX_BLOCK, :]
    gc = cumsum(ones_chunk, g)
    o_ref[0] = jnp.where(lane < AUX_GC, c, jnp.where(lane < AUX_BETA, gc, beta))


def aux_scalars(z, alog_pad, dt_pad):
    b, s, _ = z.shape
    return pl.pallas_call(
        _aux_kernel,
        grid=(b, s // AUX_BLOCK),
        in_specs=[pl.BlockSpec((1, AUX_BLOCK, AUX_LANES), lambda i, j: (i, j, 0)),
                  pl.BlockSpec((1, AUX_LANES), lambda i, j: (0, 0)),
                  pl.BlockSpec((1, AUX_LANES), lambda i, j: (0, 0))],
        out_specs=pl.BlockSpec((1, AUX_BLOCK, AUX_LANES), lambda i, j: (i, j, 0)),
        out_shape=jax.ShapeDtypeStruct((b, s, AUX_LANES), F32),
        scratch_shapes=[pltpu.VMEM((1, AUX_LANES), F32)],
        compiler_params=_params("parallel", "arbitrary"),
        name="aux_scalars",
    )(z, alog_pad, dt_pad)


def _pick_lane(x, idx):
    lane = lax.broadcasted_iota(jnp.int32, x.shape, 1)
    return jnp.sum(jnp.where(lane == idx, x, 0.0), axis=1, keepdims=True)


FOX_TQ = 512
FOX_TK = 512


def _fox_kernel(q_ref, k_ref, v_ref, ccol_ref, crow_ref, o_ref):
    h = pl.program_id(1)
    qi = pl.program_id(2)
    q = q_ref[0]
    cq = _pick_lane(ccol_ref[0], AUX_C + h)
    scale = HEAD_DIM ** -0.5

    def block(j, carry, diagonal):
        m, l, acc = carry
        start = pl.multiple_of(j * FOX_TK, FOX_TK)
        kj = k_ref[0, pl.ds(start, FOX_TK), :]
        vj = v_ref[0, pl.ds(start, FOX_TK), :]
        ck = crow_ref[0, pl.ds(j, 1), :]
        s = _dot_nt(q, kj) * scale + (cq - ck)
        if diagonal:
            ri = lax.broadcasted_iota(jnp.int32, s.shape, 0)
            ci = lax.broadcasted_iota(jnp.int32, s.shape, 1)
            s = jnp.where(ri >= ci, s, NEG_INF)
        m_new = jnp.maximum(m, jnp.max(s, axis=1, keepdims=True))
        alpha = jnp.exp(m - m_new)
        p = jnp.exp(s - m_new)
        l = alpha * l + jnp.sum(p, axis=1, keepdims=True)
        acc = alpha * acc + _dot(p.astype(BF16), vj)
        return m_new, l, acc

    init = (jnp.full((FOX_TQ, 1), NEG_INF, F32), jnp.zeros((FOX_TQ, 1), F32),
            jnp.zeros((FOX_TQ, HEAD_DIM), F32))
    carry = lax.fori_loop(0, qi, lambda j, c: block(j, c, False), init)
    _, l, acc = block(qi, carry, True)
    o_ref[0] = (acc / l).astype(o_ref.dtype)


def fox_attention(qkv, aux, c_row):
    b, s, _ = qkv.shape
    assert FOX_TQ == FOX_TK and s % FOX_TQ == 0
    nk = s // FOX_TK
    return pl.pallas_call(
        _fox_kernel,
        grid=(b, N_HEADS, s // FOX_TQ),
        in_specs=[pl.BlockSpec((1, FOX_TQ, HEAD_DIM), lambda i, h, t: (i, t, h)),
                  pl.BlockSpec((1, s, HEAD_DIM), lambda i, h, t: (i, 0, N_HEADS + h)),
                  pl.BlockSpec((1, s, HEAD_DIM), lambda i, h, t: (i, 0, 2 * N_HEADS + h)),
                  pl.BlockSpec((1, FOX_TQ, AUX_LANES), lambda i, h, t: (i, t, 0)),
                  pl.BlockSpec((1, nk, FOX_TK), lambda i, h, t: (i * N_HEADS + h, 0, 0))],
        out_specs=pl.BlockSpec((1, FOX_TQ, HEAD_DIM), lambda i, h, t: (i, t, h)),
        out_shape=jax.ShapeDtypeStruct((b, s, WIDTH), BF16),
        compiler_params=_params("parallel", "parallel", "arbitrary"),
        name="fox_attention",
    )(qkv, qkv, qkv, aux, c_row)


def _sgu_kernel(u_ref, v_ref, g_ref, b_ref, w_ref, bst_ref, o_ref):
    ri = lax.broadcasted_iota(jnp.int32, (SGU_SPAN, SGU_SPAN), 0)
    ci = lax.broadcasted_iota(jnp.int32, (SGU_SPAN, SGU_SPAN), 1)
    mask = (ri // CHUNK) >= (ci // CHUNK)
    for g in range(N_HEADS):
        cols = slice(g * HEAD_DIM, (g + 1) * HEAD_DIM)
        vn = _layer_norm_rows(v_ref[0, :, cols], g_ref[:, cols], b_ref[:, cols])
        wm = jnp.where(mask, w_ref[g], 0.0).astype(BF16)
        mixed = _dot(wm, vn.astype(BF16)) + bst_ref[:, g:g + 1]
        o_ref[0, :, cols] = (u_ref[0, :, cols] * mixed).astype(o_ref.dtype)


def spatial_gating(uv, ln_g, ln_b, w_s, b_s_t):
    b, s, _ = uv.shape
    return pl.pallas_call(
        _sgu_kernel,
        grid=(b, s // SGU_SPAN),
        in_specs=[pl.BlockSpec((1, SGU_SPAN, WIDTH), lambda i, j: (i, j, 0)),
                  pl.BlockSpec((1, SGU_SPAN, WIDTH), lambda i, j: (i, j, 1)),
                  pl.BlockSpec((1, WIDTH), lambda i, j: (0, 0)),
                  pl.BlockSpec((1, WIDTH), lambda i, j: (0, 0)),
                  pl.BlockSpec((N_HEADS, SGU_SPAN, SGU_SPAN), lambda i, j: (0, 0, 0)),
                  pl.BlockSpec((SGU_SPAN, N_HEADS), lambda i, j: (0, 0))],
        out_specs=pl.BlockSpec((1, SGU_SPAN, WIDTH), lambda i, j: (i, j, 0)),
        out_shape=jax.ShapeDtypeStruct((b, s, WIDTH), BF16),
        compiler_params=_params("parallel", "parallel"),
        name="spatial_gating",
    )(uv, uv, ln_g, ln_b, w_s, b_s_t)


GDN_BLOCK = 256
GDN_HALO = 8


def _unit_lower_inverse(a, ri, ci):
    x = jnp.where(ri == ci, 1.0, 0.0) - a
    p = _mm3(a, a)
    n = 2
    while True:
        x = x + _mm3(x, p)
        n *= 2
        if n >= CHUNK:
            return x
        p = _mm3(p, p)


def _gdn_kernel(q_ref, k_ref, v_ref, wq_ref, wk_ref, wv_ref, aux_ref, gcrow_ref, gate_ref,
                ng_ref, o_ref, qpad, kpad, vpad, state):
    h = pl.program_id(1)

    @pl.when(pl.program_id(2) == 0)
    def _():
        zeros = jnp.zeros((GDN_HALO, HEAD_DIM), F32)
        qpad[0:GDN_HALO, :] = zeros
        kpad[0:GDN_HALO, :] = zeros
        vpad[0:GDN_HALO, :] = zeros
        state[...] = jnp.zeros_like(state)

    def conv_silu(x_ref, w_ref, pad):
        pad[GDN_HALO:GDN_HALO + GDN_BLOCK, :] = x_ref[0]
        y = w_ref[GDN_CONV - 1:GDN_CONV, :] * pad[GDN_HALO:GDN_HALO + GDN_BLOCK, :]
        for j in range(GDN_CONV - 1):
            off = GDN_HALO - (GDN_CONV - 1) + j
            y = y + w_ref[j:j + 1, :] * pad[off:off + GDN_BLOCK, :]
        pad[0:GDN_HALO, :] = pad[GDN_BLOCK:GDN_BLOCK + GDN_HALO, :]
        return y * _sigmoid(y)

    def l2n(t):
        return t * lax.rsqrt(jnp.sum(t * t, axis=-1, keepdims=True) + RMS_EPS)

    q = l2n(conv_silu(q_ref, wq_ref, qpad)) * (HEAD_DIM ** -0.5)
    k = l2n(conv_silu(k_ref, wk_ref, kpad))
    v = conv_silu(v_ref, wv_ref, vpad)

    aux = aux_ref[0]
    gc_col = _pick_lane(aux, AUX_GC + h)
    beta_col = _pick_lane(aux, AUX_BETA + h)

    ri = lax.broadcasted_iota(jnp.int32, (CHUNK, CHUNK), 0)
    ci = lax.broadcasted_iota(jnp.int32, (CHUNK, CHUNK), 1)
    causal = ri >= ci
    strict = ri > ci

    for c in range(GDN_BLOCK // CHUNK):
        rows = slice(c * CHUNK, (c + 1) * CHUNK)
        qc, kc, vc = q[rows], k[rows], v[rows]
        gcc, bc = gc_col[rows], beta_col[rows]
        gcr = gcrow_ref[0, c:c + 1, :]
        decay = jnp.exp(jnp.where(causal, gcc - gcr, NEG_INF))
        kb = kc * bc
        kcb = kc.astype(BF16)
        a_kk = jnp.where(strict, _dot_nt(kb.astype(BF16), kcb) * decay, 0.0)
        t_inv = _unit_lower_inverse(a_kk, ri, ci)
        egc = jnp.exp(gcc)
        rhs = jnp.concatenate([vc * bc, kb * egc], axis=1)
        sol = _mm3(t_inv, rhs)
        u, w = sol[:, :HEAD_DIM], sol[:, HEAD_DIM:]
        qk = jnp.where(causal, _dot_nt(qc.astype(BF16), kcb) * decay, 0.0)
        g_last = gcc[CHUNK - 1:CHUNK, :]
        k_dec = kc * jnp.exp(g_last - gcc)
        q_dec = qc * egc
        st = state[...]
        stb = st.astype(BF16)
        v_new = u - _dot(w.astype(BF16), stb)
        v_newb = v_new.astype(BF16)
        o = _dot(q_dec.astype(BF16), stb) + _dot(qk.astype(BF16), v_newb)
        state[...] = st * jnp.exp(g_last) + _dot_tn(k_dec.astype(BF16), v_newb)
        o = o * lax.rsqrt(jnp.mean(o * o, axis=-1, keepdims=True) + RMS_EPS) * ng_ref[...]
        gt = gate_ref[0, rows, :]
        o_ref[0, rows, :] = (o * (gt * _sigmoid(gt))).astype(o_ref.dtype)


def gdn_mixer(qkv, conv_w, aux, gc_row, gate, norm_g):
    b, s, _ = qkv.shape
    ncb = GDN_BLOCK // CHUNK
    nblk = s // GDN_BLOCK
    gc_row = gc_row.reshape(b * N_HEADS * nblk, ncb, CHUNK)
    x_spec = lambda off: pl.BlockSpec((1, GDN_BLOCK, HEAD_DIM), lambda i, h, t: (i, t, off + h))
    w_spec = lambda off: pl.BlockSpec((GDN_CONV, HEAD_DIM), lambda i, h, t: (0, off + h))
    pad = pltpu.VMEM((GDN_BLOCK + GDN_HALO, HEAD_DIM), F32)
    return pl.pallas_call(
        _gdn_kernel,
        grid=(b, N_HEADS, s // GDN_BLOCK),
        in_specs=[x_spec(0), x_spec(N_HEADS), x_spec(2 * N_HEADS),
                  w_spec(0), w_spec(N_HEADS), w_spec(2 * N_HEADS),
                  pl.BlockSpec((1, GDN_BLOCK, AUX_LANES), lambda i, h, t: (i, t, 0)),
                  pl.BlockSpec((1, ncb, CHUNK), lambda i, h, t: ((i * N_HEADS + h) * nblk + t, 0, 0)),
                  pl.BlockSpec((1, GDN_BLOCK, HEAD_DIM), lambda i, h, t: (i, t, h)),
                  pl.BlockSpec((1, HEAD_DIM), lambda i, h, t: (0, 0))],
        out_specs=pl.BlockSpec((1, GDN_BLOCK, HEAD_DIM), lambda i, h, t: (i, t, h)),
        out_shape=jax.ShapeDtypeStruct((b, s, WIDTH), BF16),
        scratch_shapes=[pad, pad, pad, pltpu.VMEM((HEAD_DIM, HEAD_DIM), F32)],
        compiler_params=_params("parallel", "parallel", "arbitrary"),
        name="gdn_mixer",
    )(qkv, qkv, qkv, conv_w, conv_w, conv_w, aux, gc_row, gate, norm_g)


MERGE_TM = 512
MERGE_TN = 512


def _merge_kernel(xb_ref, ya_ref, yb_ref, yc_ref, wg0, wg1, wg2, bg0, bg1, bg2, wa, wb, wc, o_ref):
    xb = xb_ref[...]

    def branch(y_ref, wp, wg, bg):
        return _sigmoid(_dot(xb, wg[...]) + bg[...]) * _dot(y_ref[...], wp[...])

    merged = branch(ya_ref, wa, wg0, bg0) + branch(yb_ref, wb, wg1, bg1) + branch(yc_ref, wc, wg2, bg2)
    o_ref[...] = merged.astype(o_ref.dtype)


def merge_branches(xb, ya, yb, yc, wg, bg, wp):
    m = xb.shape[0]
    row = lambda width: pl.BlockSpec((MERGE_TM, width), lambda i, j: (i, 0))
    col = lambda depth: pl.BlockSpec((depth, MERGE_TN), lambda i, j: (0, j))
    return pl.pallas_call(
        _merge_kernel,
        grid=(m // MERGE_TM, D_MODEL // MERGE_TN),
        in_specs=[row(D_MODEL), row(WIDTH), row(WIDTH), row(WIDTH),
                  col(D_MODEL), col(D_MODEL), col(D_MODEL), col(1), col(1), col(1),
                  col(WIDTH), col(WIDTH), col(WIDTH)],
        out_specs=pl.BlockSpec((MERGE_TM, MERGE_TN), lambda i, j: (i, j)),
        out_shape=jax.ShapeDtypeStruct((m, D_MODEL), BF16),
        compiler_params=_params("parallel", "arbitrary"),
        name="merge_branches",
    )(xb, ya, yb, yc, *wg, *bg, *wp)


OUT_TM = 256


def _proj_ln_kernel(y_ref, w_ref, x_ref, g_ref, b_ref, o_ref, ob_ref):
    r = DEEPNORM_ALPHA * x_ref[...] + _dot(y_ref[...], w_ref[...])
    out = _layer_norm_rows(r, g_ref[...], b_ref[...])
    o_ref[...] = out
    ob_ref[...] = out.astype(BF16)


def proj_residual_ln(y, w, x, g, b):
    m, k = y.shape
    vec = pl.BlockSpec((1, D_MODEL), lambda i: (0, 0))
    return pl.pallas_call(
        _proj_ln_kernel,
        grid=(m // OUT_TM,),
        in_specs=[pl.BlockSpec((OUT_TM, k), lambda i: (i, 0)),
                  pl.BlockSpec((k, D_MODEL), lambda i: (0, 0)),
                  pl.BlockSpec((OUT_TM, D_MODEL), lambda i: (i, 0)), vec, vec],
        out_specs=[pl.BlockSpec((OUT_TM, D_MODEL), lambda i: (i, 0)),
                   pl.BlockSpec((OUT_TM, D_MODEL), lambda i: (i, 0))],
        out_shape=[jax.ShapeDtypeStruct((m, D_MODEL), F32), jax.ShapeDtypeStruct((m, D_MODEL), BF16)],
        compiler_params=_params("parallel"),
        name="proj_residual_ln",
    )(y, w, x, g, b)


FFN_TM = 1024
FFN_TN = 512
FFN_HALO = 16


def _ffn_up_kernel(x_ref, halo_ref, wg_ref, wv_ref, cwg_ref, cwv_ref, cbg_ref, cbv_ref, o_ref, xs_ref,
                   *, tiles_per_seq):
    @pl.when(pl.program_id(1) == 0)
    def _():
        first = (pl.program_id(0) % tiles_per_seq) == 0
        halo = halo_ref[...]
        xs_ref[0:FFN_HALO, :] = jnp.where(first, jnp.zeros_like(halo), halo)
        xs_ref[FFN_HALO:FFN_HALO + FFN_TM, :] = x_ref[...]

    xs = xs_ref[...]

    def conv(w_ref, cw_ref, cb_ref):
        r = _dot(xs, w_ref[...])
        y = cb_ref[...] + cw_ref[FFN_CONV - 1:FFN_CONV, :] * r[FFN_HALO:FFN_HALO + FFN_TM]
        for j in range(FFN_CONV - 1):
            off = FFN_HALO - (FFN_CONV - 1) + j
            y = y + cw_ref[j:j + 1, :] * r[off:off + FFN_TM]
        return y

    hg = conv(wg_ref, cwg_ref, cbg_ref)
    hv = conv(wv_ref, cwv_ref, cbv_ref)
    o_ref[...] = (hg * _sigmoid(hg) * hv).astype(o_ref.dtype)


def ffn_up(xb, seq_len, wg, wv, cwg, cwv, cbg, cbv):
    m, k = xb.shape
    n = wg.shape[1]
    assert seq_len % FFN_TM == 0
    halo_blocks = FFN_TM // FFN_HALO
    col = lambda depth: pl.BlockSpec((depth, FFN_TN), lambda i, j: (0, j))
    return pl.pallas_call(
        functools.partial(_ffn_up_kernel, tiles_per_seq=seq_len // FFN_TM),
        grid=(m // FFN_TM, n // FFN_TN),
        in_specs=[pl.BlockSpec((FFN_TM, k), lambda i, j: (i, 0)),
                  pl.BlockSpec((FFN_HALO, k), lambda i, j: (jnp.maximum(i * halo_blocks - 1, 0), 0)),
                  col(k), col(k), col(FFN_CONV), col(FFN_CONV), col(1), col(1)],
        out_specs=pl.BlockSpec((FFN_TM, FFN_TN), lambda i, j: (i, j)),
        out_shape=jax.ShapeDtypeStruct((m, n), BF16),
        scratch_shapes=[pltpu.VMEM((FFN_HALO + FFN_TM, k), BF16)],
        compiler_params=_params("parallel", "arbitrary"),
        name="ffn_up",
    )(xb, xb, wg, wv, cwg, cwv, cbg, cbv)


DOWN_TM = 512
DOWN_TK = 512


def _ffn_down_kernel(a_ref, w_ref, x_ref, g_ref, b_ref, o_ref, ob_ref, acc_ref):
    kk = pl.program_id(1)

    @pl.when(kk == 0)
    def _():
        acc_ref[...] = DEEPNORM_ALPHA * x_ref[...]

    acc_ref[...] += _dot(a_ref[...], w_ref[...])

    @pl.when(kk == pl.num_programs(1) - 1)
    def _():
        out = _layer_norm_rows(acc_ref[...], g_ref[...], b_ref[...])
        o_ref[...] = out
        ob_ref[...] = out.astype(BF16)


def ffn_down_ln(act, w, x, g, b):
    m, k = act.shape
    vec = pl.BlockSpec((1, D_MODEL), lambda i, j: (0, 0))
    row = pl.BlockSpec((DOWN_TM, D_MODEL), lambda i, j: (i, 0))
    return pl.pallas_call(
        _ffn_down_kernel,
        grid=(m // DOWN_TM, k // DOWN_TK),
        in_specs=[pl.BlockSpec((DOWN_TM, DOWN_TK), lambda i, j: (i, j)),
                  pl.BlockSpec((DOWN_TK, D_MODEL), lambda i, j: (j, 0)),
                  row, vec, vec],
        out_specs=[row, row],
        out_shape=[jax.ShapeDtypeStruct((m, D_MODEL), F32), jax.ShapeDtypeStruct((m, D_MODEL), BF16)],
        scratch_shapes=[pltpu.VMEM((DOWN_TM, D_MODEL), F32)],
        compiler_params=_params("parallel", "arbitrary"),
        name="ffn_down_ln",
    )(act, w, x, g, b)


_FOX_QKV = (0, 3 * WIDTH)
_FOX_F = (_FOX_QKV[1], _FOX_QKV[1] + N_HEADS)
_SGU_UV = (_FOX_F[1], _FOX_F[1] + 2 * WIDTH)
_GDN_QKV = (_SGU_UV[1], _SGU_UV[1] + 3 * WIDTH)
_GDN_A = (_GDN_QKV[1], _GDN_QKV[1] + N_HEADS)
_GDN_B = (_GDN_A[1], _GDN_A[1] + N_HEADS)
_GDN_GATE = (_GDN_B[1], _GDN_B[1] + WIDTH)
_GATES = (_GDN_GATE[1], _GDN_GATE[1] + 3 * D_MODEL)


def _cols(w, span):
    return w[..., span[0]:span[1]]


def _pad_lanes(v, offset):
    return jnp.zeros((1, AUX_LANES), F32).at[0, offset:offset + v.shape[0]].set(v)


def _layer(x, xb, bsz, seq, w_in, b_in, sgu_ln_g, sgu_ln_b, sgu_w, sgu_b, gdn_conv_w, gdn_a_log,
           gdn_dt_bias, gdn_norm_g, w_proj_a, w_proj_b, w_proj_c, w_out, ln1_g, ln1_b,
           ffn_w_up, ffn_conv_w, ffn_conv_b, ffn_w_down, ln2_g, ln2_b):
    tokens = bsz * seq
    wb = lambda span: _cols(w_in, span).astype(BF16)
    bias = lambda span: _cols(b_in, span)[None, :]

    fox_qkv = matmul_bias(xb, wb(_FOX_QKV), bias(_FOX_QKV), BF16, 1024, 512, "in_fox")
    sgu_uv = matmul_bias(xb, wb(_SGU_UV), bias(_SGU_UV), F32, 1024, 512, "in_sgu")
    gdn_qkv = matmul_bias(xb, wb(_GDN_QKV), bias(_GDN_QKV), F32, 1024, 512, "in_gdn")
    gdn_gate = matmul_bias(xb, wb(_GDN_GATE), bias(_GDN_GATE), F32, 1024, 512, "in_gdn_gate")
    pad = AUX_LANES - 3 * N_HEADS
    w_small = jnp.concatenate([_cols(w_in, _FOX_F), _cols(w_in, _GDN_A), _cols(w_in, _GDN_B),
                               jnp.zeros((D_MODEL, pad), F32)], axis=1).astype(BF16)
    b_small = jnp.concatenate([_cols(b_in, _FOX_F), _cols(b_in, _GDN_A), _cols(b_in, _GDN_B),
                               jnp.zeros((pad,), F32)])[None, :]
    z_small = matmul_bias(xb, w_small, b_small, F32, 1024, AUX_LANES, "in_small")

    aux = aux_scalars(z_small.reshape(bsz, seq, AUX_LANES), _pad_lanes(gdn_a_log, AUX_GC),
                      _pad_lanes(gdn_dt_bias, AUX_GC))
    head_rows = lambda off: jnp.swapaxes(aux[:, :, off:off + N_HEADS], 1, 2).reshape(bsz * N_HEADS, seq)
    c_row = head_rows(AUX_C).reshape(bsz * N_HEADS, seq // FOX_TK, FOX_TK)
    gc_row = head_rows(AUX_GC)

    y_a = fox_attention(fox_qkv.reshape(bsz, seq, 3 * WIDTH), aux, c_row)
    y_b = spatial_gating(sgu_uv.reshape(bsz, seq, 2 * WIDTH), sgu_ln_g[None, :], sgu_ln_b[None, :],
                         sgu_w, sgu_b.T)
    y_c = gdn_mixer(gdn_qkv.reshape(bsz, seq, 3 * WIDTH), gdn_conv_w, aux, gc_row,
                    gdn_gate.reshape(bsz, seq, WIDTH), gdn_norm_g[None, :])

    g0 = _GATES[0]
    wg = [w_in[:, g0 + i * D_MODEL:g0 + (i + 1) * D_MODEL].astype(BF16) for i in range(3)]
    bg = [b_in[None, g0 + i * D_MODEL:g0 + (i + 1) * D_MODEL] for i in range(3)]
    wp = [w_proj_a.astype(BF16), w_proj_b.astype(BF16), w_proj_c.astype(BF16)]
    merged = merge_branches(xb, y_a.reshape(tokens, WIDTH), y_b.reshape(tokens, WIDTH),
                            y_c.reshape(tokens, WIDTH), wg, bg, wp)
    x, xb = proj_residual_ln(merged, w_out.astype(BF16), x, ln1_g[None, :], ln1_b[None, :])

    ff_pad = D_FF_PAD - D_FF
    pad_cols = lambda t: jnp.pad(t, ((0, 0), (0, ff_pad)))
    act = ffn_up(xb, seq,
                 pad_cols(ffn_w_up[:, :D_FF]).astype(BF16), pad_cols(ffn_w_up[:, D_FF:]).astype(BF16),
                 pad_cols(ffn_conv_w[:, :D_FF]), pad_cols(ffn_conv_w[:, D_FF:]),
                 pad_cols(ffn_conv_b[None, :D_FF]), pad_cols(ffn_conv_b[None, D_FF:]))
    w_down = jnp.pad(ffn_w_down, ((0, ff_pad), (0, 0))).astype(BF16)
    return ffn_down_ln(act, w_down, x, ln2_g[None, :], ln2_b[None, :])


@jax.jit
def kernel(x, w_in, b_in, sgu_ln_g, sgu_ln_b, sgu_w, sgu_b, gdn_conv_w, gdn_a_log, gdn_dt_bias, gdn_norm_g, w_proj_a, w_proj_b, w_proj_c, w_out, ln1_g, ln1_b, ffn_w_up, ffn_conv_w, ffn_conv_b, ffn_w_down, ln2_g, ln2_b):
    bsz, seq, _ = x.shape
    stacked = (w_in, b_in, sgu_ln_g, sgu_ln_b, sgu_w, sgu_b, gdn_conv_w, gdn_a_log, gdn_dt_bias,
               gdn_norm_g, w_proj_a, w_proj_b, w_proj_c, w_out, ln1_g, ln1_b, ffn_w_up, ffn_conv_w,
               ffn_conv_b, ffn_w_down, ln2_g, ln2_b)
    x = x.reshape(bsz * seq, D_MODEL)
    xb = x.astype(BF16)
    for layer in range(w_in.shape[0]):
        x, xb = _layer(x, xb, bsz, seq, *(t[layer] for t in stacked))
    return x.reshape(bsz, seq, D_MODEL)
```

```python
import functools

import jax
import jax.numpy as jnp
from jax import lax
from jax.experimental import pallas as pl
from jax.experimental.pallas import tpu as pltpu

F32 = jnp.float32
BF16 = jnp.bfloat16

D_MODEL = 2048
HEAD_DIM = 128
N_HEADS = 8
WIDTH = N_HEADS * HEAD_DIM
CHUNK = 64
SGU_SPAN = 128
GDN_CONV = 4
D_FF = 5504
D_FF_PAD = 5632
FFN_CONV = 3
DEPTH = 4
DEEPNORM_ALPHA = (2 * DEPTH) ** 0.25
LN_EPS = 1e-5
RMS_EPS = 1e-6

AUX_LANES = 128
AUX_C = 0
AUX_GC = 8
AUX_BETA = 16

VMEM_LIMIT = 56 * 1024 * 1024

NEG_INF = float("-inf")


def _params(*semantics):
    return pltpu.CompilerParams(dimension_semantics=semantics, vmem_limit_bytes=VMEM_LIMIT)


def _dot(a, b):
    return jnp.dot(a, b, preferred_element_type=F32)


def _dot_nt(a, b):
    return lax.dot_general(a, b, (((1,), (1,)), ((), ())), preferred_element_type=F32)


def _dot_tn(a, b):
    return lax.dot_general(a, b, (((0,), (0,)), ((), ())), preferred_element_type=F32)


def _split2(x):
    hi = x.astype(BF16)
    lo = (x - hi.astype(F32)).astype(BF16)
    return hi, lo


def _split3(x):
    hi = x.astype(BF16)
    r1 = x - hi.astype(F32)
    mid = r1.astype(BF16)
    lo = (r1 - mid.astype(F32)).astype(BF16)
    return hi, mid, lo


def _mm3(a, b):
    ah, al = _split2(a)
    bh, bl = _split2(b)
    return _dot(ah, bh) + _dot(ah, bl) + _dot(al, bh)


def _softplus(z):
    return jnp.maximum(z, 0.0) + jnp.log1p(jnp.exp(-jnp.abs(z)))


def _sigmoid(z):
    return 1.0 / (1.0 + jnp.exp(-z))


def _layer_norm_rows(y, g, b):
    mu = jnp.mean(y, axis=-1, keepdims=True)
    yc = y - mu
    var = jnp.mean(yc * yc, axis=-1, keepdims=True)
    return yc * lax.rsqrt(var + LN_EPS) * g + b


def _mm_bias_kernel(x_ref, w_ref, b_ref, o_ref):
    acc = _dot(x_ref[...], w_ref[...])
    o_ref[...] = (acc + b_ref[...]).astype(o_ref.dtype)


def matmul_bias(xb, w, b, out_dtype, tm, tn, name):
    m, k = xb.shape
    n = w.shape[1]
    return pl.pallas_call(
        _mm_bias_kernel,
        grid=(m // tm, n // tn),
        in_specs=[pl.BlockSpec((tm, k), lambda i, j: (i, 0)),
                  pl.BlockSpec((k, tn), lambda i, j: (0, j)),
                  pl.BlockSpec((1, tn), lambda i, j: (0, j))],
        out_specs=pl.BlockSpec((tm, tn), lambda i, j: (i, j)),
        out_shape=jax.ShapeDtypeStruct((m, n), out_dtype),
        compiler_params=_params("parallel", "arbitrary"),
        name=name,
    )(xb, w, b)


AUX_BLOCK = 256


def _aux_kernel(z_ref, alog_ref, dt_ref, o_ref, carry_ref):
    @pl.when(pl.program_id(1) == 0)
    def _():
        carry_ref[...] = jnp.zeros_like(carry_ref)

    z = z_ref[0]
    lane = lax.broadcasted_iota(jnp.int32, z.shape, 1)
    log_f = -_softplus(-z)
    g = -jnp.exp(alog_ref[...]) * _softplus(z + dt_ref[...])
    beta = _sigmoid(z)

    ri = lax.broadcasted_iota(jnp.int32, (AUX_BLOCK, AUX_BLOCK), 0)
    ci = lax.broadcasted_iota(jnp.int32, (AUX_BLOCK, AUX_BLOCK), 1)
    tri = ri >= ci
    ones_seq = jnp.where(tri, 1.0, 0.0).astype(BF16)
    ones_chunk = jnp.where(tri & ((ri // CHUNK) == (ci // CHUNK)), 1.0, 0.0).astype(BF16)

    def cumsum(ones, val):
        hi, mid, lo = _split3(val)
        return _dot(ones, hi) + _dot(ones, mid) + _dot(ones, lo)

    c = cumsum(ones_seq, log_f) + carry_ref[...]
    carry_ref[...] = c[AUX_BLOCK - 1:AUX_BLOCK, :]
    gc = cumsum(ones_chunk, g)
    o_ref[0] = jnp.where(lane < AUX_GC, c, jnp.where(lane < AUX_BETA, gc, beta))


def aux_scalars(z, alog_pad, dt_pad):
    b, s, _ = z.shape
    return pl.pallas_call(
        _aux_kernel,
        grid=(b, s // AUX_BLOCK),
        in_specs=[pl.BlockSpec((1, AUX_BLOCK, AUX_LANES), lambda i, j: (i, j, 0)),
                  pl.BlockSpec((1, AUX_LANES), lambda i, j: (0, 0)),
                  pl.BlockSpec((1, AUX_LANES), lambda i, j: (0, 0))],
        out_specs=pl.BlockSpec((1, AUX_BLOCK, AUX_LANES), lambda i, j: (i, j, 0)),
        out_shape=jax.ShapeDtypeStruct((b, s, AUX_LANES), F32),
        scratch_shapes=[pltpu.VMEM((1, AUX_LANES), F32)],
        compiler_params=_params("parallel", "arbitrary"),
        name="aux_scalars",
    )(z, alog_pad, dt_pad)


def _pick_lane(x, idx):
    lane = lax.broadcasted_iota(jnp.int32, x.shape, 1)
    return jnp.sum(jnp.where(lane == idx, x, 0.0), axis=1, keepdims=True)


FOX_TQ = 512
FOX_TK = 512


def _fox_kernel(q_ref, k_ref, v_ref, ccol_ref, crow_ref, o_ref):
    h = pl.program_id(1)
    qi = pl.program_id(2)
    q = q_ref[0]
    cq = _pick_lane(ccol_ref[0], AUX_C + h)
    scale = HEAD_DIM ** -0.5

    def block(j, carry, diagonal):
        m, l, acc = carry
        start = pl.multiple_of(j * FOX_TK, FOX_TK)
        kj = k_ref[0, pl.ds(start, FOX_TK), :]
        vj = v_ref[0, pl.ds(start, FOX_TK), :]
        ck = crow_ref[0, pl.ds(j, 1), :]
        s = _dot_nt(q, kj) * scale + (cq - ck)
        if diagonal:
            ri = lax.broadcasted_iota(jnp.int32, s.shape, 0)
            ci = lax.broadcasted_iota(jnp.int32, s.shape, 1)
            s = jnp.where(ri >= ci, s, NEG_INF)
        m_new = jnp.maximum(m, jnp.max(s, axis=1, keepdims=True))
        alpha = jnp.exp(m - m_new)
        p = jnp.exp(s - m_new)
        l = alpha * l + jnp.sum(p, axis=1, keepdims=True)
        acc = alpha * acc + _dot(p.astype(BF16), vj)
        return m_new, l, acc

    init = (jnp.full((FOX_TQ, 1), NEG_INF, F32), jnp.zeros((FOX_TQ, 1), F32),
            jnp.zeros((FOX_TQ, HEAD_DIM), F32))
    carry = lax.fori_loop(0, qi, lambda j, c: block(j, c, False), init)
    _, l, acc = block(qi, carry, True)
    o_ref[0] = (acc / l).astype(o_ref.dtype)


def fox_attention(qkv, aux, c_row):
    b, s, _ = qkv.shape
    assert FOX_TQ == FOX_TK and s % FOX_TQ == 0
    nk = s // FOX_TK
    return pl.pallas_call(
        _fox_kernel,
        grid=(b, N_HEADS, s // FOX_TQ),
        in_specs=[pl.BlockSpec((1, FOX_TQ, HEAD_DIM), lambda i, h, t: (i, t, h)),
                  pl.BlockSpec((1, s, HEAD_DIM), lambda i, h, t: (i, 0, N_HEADS + h)),
                  pl.BlockSpec((1, s, HEAD_DIM), lambda i, h, t: (i, 0, 2 * N_HEADS + h)),
                  pl.BlockSpec((1, FOX_TQ, AUX_LANES), lambda i, h, t: (i, t, 0)),
                  pl.BlockSpec((1, nk, FOX_TK), lambda i, h, t: (i * N_HEADS + h, 0, 0))],
        out_specs=pl.BlockSpec((1, FOX_TQ, HEAD_DIM), lambda i, h, t: (i, t, h)),
        out_shape=jax.ShapeDtypeStruct((b, s, WIDTH), BF16),
        compiler_params=_params("parallel", "parallel", "arbitrary"),
        name="fox_attention",
    )(qkv, qkv, qkv, aux, c_row)


def _sgu_kernel(u_ref, v_ref, g_ref, b_ref, w_ref, bst_ref, o_ref):
    ri = lax.broadcasted_iota(jnp.int32, (SGU_SPAN, SGU_SPAN), 0)
    ci = lax.broadcasted_iota(jnp.int32, (SGU_SPAN, SGU_SPAN), 1)
    mask = (ri // CHUNK) >= (ci // CHUNK)
    for g in range(N_HEADS):
        cols = slice(g * HEAD_DIM, (g + 1) * HEAD_DIM)
        vn = _layer_norm_rows(v_ref[0, :, cols], g_ref[:, cols], b_ref[:, cols])
        wm = jnp.where(mask, w_ref[g], 0.0).astype(BF16)
        mixed = _dot(wm, vn.astype(BF16)) + bst_ref[:, g:g + 1]
        o_ref[0, :, cols] = (u_ref[0, :, cols] * mixed).astype(o_ref.dtype)


def spatial_gating(uv, ln_g, ln_b, w_s, b_s_t):
    b, s, _ = uv.shape
    return pl.pallas_call(
        _sgu_kernel,
        grid=(b, s // SGU_SPAN),
        in_specs=[pl.BlockSpec((1, SGU_SPAN, WIDTH), lambda i, j: (i, j, 0)),
                  pl.BlockSpec((1, SGU_SPAN, WIDTH), lambda i, j: (i, j, 1)),
                  pl.BlockSpec((1, WIDTH), lambda i, j: (0, 0)),
                  pl.BlockSpec((1, WIDTH), lambda i, j: (0, 0)),
                  pl.BlockSpec((N_HEADS, SGU_SPAN, SGU_SPAN), lambda i, j: (0, 0, 0)),
                  pl.BlockSpec((SGU_SPAN, N_HEADS), lambda i, j: (0, 0))],
        out_specs=pl.BlockSpec((1, SGU_SPAN, WIDTH), lambda i, j: (i, j, 0)),
        out_shape=jax.ShapeDtypeStruct((b, s, WIDTH), BF16),
        compiler_params=_params("parallel", "parallel"),
        name="spatial_gating",
    )(uv, uv, ln_g, ln_b, w_s, b_s_t)


GDN_PAIR = 2 * CHUNK
GDN_HALO = 8


def _gdn_prep_kernel(x_ref, halo_ref, cw_ref, aux_ref, gcrow_ref, u_ref, w_ref, qd_ref, kdt_ref, qk_ref,
                     pad_ref):
    first = pl.program_id(1) == 0
    halo = halo_ref[0]
    pad_ref[0:GDN_HALO, :] = jnp.where(first, jnp.zeros_like(halo), halo)
    pad_ref[GDN_HALO:GDN_HALO + GDN_PAIR, :] = x_ref[0]

    def conv_silu(cols):
        y = cw_ref[GDN_CONV - 1:GDN_CONV, cols] * pad_ref[GDN_HALO:GDN_HALO + GDN_PAIR, cols]
        for j in range(GDN_CONV - 1):
            off = GDN_HALO - (GDN_CONV - 1) + j
            y = y + cw_ref[j:j + 1, cols] * pad_ref[off:off + GDN_PAIR, cols]
        return y * _sigmoid(y)

    def l2n(t):
        return t * lax.rsqrt(jnp.sum(t * t, axis=-1, keepdims=True) + RMS_EPS)

    ri = lax.broadcasted_iota(jnp.int32, (GDN_PAIR, GDN_PAIR), 0)
    ci = lax.broadcasted_iota(jnp.int32, (GDN_PAIR, GDN_PAIR), 1)
    same_chunk = (ri // CHUNK) == (ci // CHUNK)
    causal = same_chunk & (ri >= ci)
    strict = same_chunk & (ri > ci)
    row = lax.broadcasted_iota(jnp.int32, (GDN_PAIR, 1), 0)
    aux = aux_ref[0]

    heads = range(N_HEADS)
    head_cols = [slice(h * HEAD_DIM, (h + 1) * HEAD_DIM) for h in heads]
    ys, ps, rhss = [], [], []
    for h in heads:
        cols = head_cols[h]
        q = l2n(conv_silu(slice(h * HEAD_DIM, (h + 1) * HEAD_DIM))) * (HEAD_DIM ** -0.5)
        k = l2n(conv_silu(slice(WIDTH + h * HEAD_DIM, WIDTH + (h + 1) * HEAD_DIM)))
        v = conv_silu(slice(2 * WIDTH + h * HEAD_DIM, 2 * WIDTH + (h + 1) * HEAD_DIM))
        gc = aux[:, AUX_GC + h:AUX_GC + h + 1]
        beta = aux[:, AUX_BETA + h:AUX_BETA + h + 1]
        gc_r = gcrow_ref[0, h:h + 1, :]
        decay = jnp.exp(jnp.where(causal, gc - gc_r, NEG_INF))
        kb = k * beta
        kbf = k.astype(BF16)
        prod = _dot_nt(jnp.concatenate([kb, q], axis=0).astype(BF16), kbf)
        a = jnp.where(strict, prod[:GDN_PAIR] * decay, 0.0)
        qk_ref[0, :, cols] = (prod[GDN_PAIR:] * decay).astype(BF16)
        egc = jnp.exp(gc)
        rhss.append(jnp.concatenate([v * beta, kb * egc], axis=1))
        qd_ref[0, :, cols] = (q * egc).astype(BF16)
        g_last = jnp.where(row < CHUNK, gc[CHUNK - 1:CHUNK, :], gc[GDN_PAIR - 1:GDN_PAIR, :])
        kdt_ref[0, cols, :] = (k * jnp.exp(g_last - gc)).T.astype(BF16)
        ab = a.astype(BF16)
        ys.append(-a)
        ps.append(_dot(ab, ab))

    n = 2
    while True:
        pbs = [p.astype(BF16) for p in ps]
        ys = [ys[h] + ps[h] + _dot(ys[h].astype(BF16), pbs[h]) for h in heads]
        n *= 2
        if n >= CHUNK:
            break
        ps = [_dot(pb, pb) for pb in pbs]

    for h in heads:
        sol = rhss[h] + _dot(ys[h].astype(BF16), rhss[h].astype(BF16))
        u_ref[0, :, head_cols[h]] = sol[:, :HEAD_DIM]
        w_ref[0, :, head_cols[h]] = sol[:, HEAD_DIM:].astype(BF16)


def gdn_prepare(qkv, conv_w, aux, gc_row):
    b, s, _ = qkv.shape
    halo_blocks = GDN_PAIR // GDN_HALO
    tok = pl.BlockSpec((1, GDN_PAIR, WIDTH), lambda i, t: (i, t, 0))
    wide = lambda dt: jax.ShapeDtypeStruct((b, s, WIDTH), dt)
    return pl.pallas_call(
        _gdn_prep_kernel,
        grid=(b, s // GDN_PAIR),
        in_specs=[pl.BlockSpec((1, GDN_PAIR, 3 * WIDTH), lambda i, t: (i, t, 0)),
                  pl.BlockSpec((1, GDN_HALO, 3 * WIDTH),
                               lambda i, t: (i, jnp.maximum(t * halo_blocks - 1, 0), 0)),
                  pl.BlockSpec((GDN_CONV, 3 * WIDTH), lambda i, t: (0, 0)),
                  pl.BlockSpec((1, GDN_PAIR, AUX_LANES), lambda i, t: (i, t, 0)),
                  pl.BlockSpec((1, N_HEADS, GDN_PAIR), lambda i, t: (i, 0, t))],
        out_specs=[tok, tok, tok, pl.BlockSpec((1, WIDTH, GDN_PAIR), lambda i, t: (i, 0, t)), tok],
        out_shape=[wide(F32), wide(BF16), wide(BF16), jax.ShapeDtypeStruct((b, WIDTH, s), BF16), wide(BF16)],
        scratch_shapes=[pltpu.VMEM((GDN_HALO + GDN_PAIR, 3 * WIDTH), F32)],
        compiler_params=_params("parallel", "parallel"),
        name="gdn_prepare",
    )(qkv, qkv, conv_w, aux, gc_row)


SCAN_BLOCK = 256


def _gdn_scan_kernel(u_ref, w_ref, qd_ref, qk_ref, kdt_ref, aux_ref, gate_ref, ng_ref, o_ref, state_ref):
    @pl.when(pl.program_id(1) == 0)
    def _():
        state_ref[...] = jnp.zeros_like(state_ref)

    zeros = jnp.zeros((CHUNK, HEAD_DIM), BF16)
    for c in range(SCAN_BLOCK // CHUNK):
        rows = slice(c * CHUNK, (c + 1) * CHUNK)
        pair = slice((c // 2) * GDN_PAIR, (c // 2 + 1) * GDN_PAIR)
        last = (c + 1) * CHUNK - 1
        for h in range(N_HEADS):
            cols = slice(h * HEAD_DIM, (h + 1) * HEAD_DIM)
            st = state_ref[h]
            r1 = _dot(jnp.concatenate([w_ref[0, rows, cols], qd_ref[0, rows, cols]], axis=0),
                      st.astype(BF16))
            v_new = (u_ref[0, rows, cols] - r1[:CHUNK]).astype(BF16)
            v_pad = jnp.concatenate([v_new, zeros] if c % 2 == 0 else [zeros, v_new], axis=0)
            r2 = _dot(jnp.concatenate([qk_ref[0, rows, cols], kdt_ref[0, cols, pair]], axis=0), v_pad)
            decay_last = jnp.exp(aux_ref[0, last:last + 1, AUX_GC + h:AUX_GC + h + 1])
            state_ref[h] = st * decay_last + r2[CHUNK:]
            o = r1[CHUNK:] + r2[:CHUNK]
            o = o * lax.rsqrt(jnp.mean(o * o, axis=-1, keepdims=True) + RMS_EPS) * ng_ref[...]
            gt = gate_ref[0, rows, cols]
            o_ref[0, rows, cols] = (o * (gt * _sigmoid(gt))).astype(o_ref.dtype)


def gdn_scan(u, w, qd, qk, kdt, aux, gate, norm_g):
    b, s, _ = u.shape
    tok = pl.BlockSpec((1, SCAN_BLOCK, WIDTH), lambda i, t: (i, t, 0))
    return pl.pallas_call(
        _gdn_scan_kernel,
        grid=(b, s // SCAN_BLOCK),
        in_specs=[tok, tok, tok, tok,
                  pl.BlockSpec((1, WIDTH, SCAN_BLOCK), lambda i, t: (i, 0, t)),
                  pl.BlockSpec((1, SCAN_BLOCK, AUX_LANES), lambda i, t: (i, t, 0)),
                  tok, pl.BlockSpec((1, HEAD_DIM), lambda i, t: (0, 0))],
        out_specs=tok,
        out_shape=jax.ShapeDtypeStruct((b, s, WIDTH), BF16),
        scratch_shapes=[pltpu.VMEM((N_HEADS, HEAD_DIM, HEAD_DIM), F32)],
        compiler_params=_params("parallel", "arbitrary"),
        name="gdn_scan",
    )(u, w, qd, qk, kdt, aux, gate, norm_g)


def gdn_mixer(qkv, conv_w, aux, gc_row, gate, norm_g):
    u, w, qd, kdt, qk = gdn_prepare(qkv, conv_w, aux, gc_row)
    return gdn_scan(u, w, qd, qk, kdt, aux, gate, norm_g)


MERGE_TM = 512
MERGE_TN = 512


def _merge_kernel(xb_ref, ya_ref, yb_ref, yc_ref, wg0, wg1, wg2, bg0, bg1, bg2, wa, wb, wc, o_ref):
    xb = xb_ref[...]

    def branch(y_ref, wp, wg, bg):
        return _sigmoid(_dot(xb, wg[...]) + bg[...]) * _dot(y_ref[...], wp[...])

    merged = branch(ya_ref, wa, wg0, bg0) + branch(yb_ref, wb, wg1, bg1) + branch(yc_ref, wc, wg2, bg2)
    o_ref[...] = merged.astype(o_ref.dtype)


def merge_branches(xb, ya, yb, yc, wg, bg, wp):
    m = xb.shape[0]
    row = lambda width: pl.BlockSpec((MERGE_TM, width), lambda i, j: (i, 0))
    col = lambda depth: pl.BlockSpec((depth, MERGE_TN), lambda i, j: (0, j))
    return pl.pallas_call(
        _merge_kernel,
        grid=(m // MERGE_TM, D_MODEL // MERGE_TN),
        in_specs=[row(D_MODEL), row(WIDTH), row(WIDTH), row(WIDTH),
                  col(D_MODEL), col(D_MODEL), col(D_MODEL), col(1), col(1), col(1),
                  col(WIDTH), col(WIDTH), col(WIDTH)],
        out_specs=pl.BlockSpec((MERGE_TM, MERGE_TN), lambda i, j: (i, j)),
        out_shape=jax.ShapeDtypeStruct((m, D_MODEL), BF16),
        compiler_params=_params("parallel", "arbitrary"),
        name="merge_branches",
    )(xb, ya, yb, yc, *wg, *bg, *wp)


OUT_TM = 256


def _proj_ln_kernel(y_ref, w_ref, x_ref, g_ref, b_ref, o_ref, ob_ref):
    r = DEEPNORM_ALPHA * x_ref[...] + _dot(y_ref[...], w_ref[...])
    out = _layer_norm_rows(r, g_ref[...], b_ref[...])
    o_ref[...] = out
    ob_ref[...] = out.astype(BF16)


def proj_residual_ln(y, w, x, g, b):
    m, k = y.shape
    vec = pl.BlockSpec((1, D_MODEL), lambda i: (0, 0))
    return pl.pallas_call(
        _proj_ln_kernel,
        grid=(m // OUT_TM,),
        in_specs=[pl.BlockSpec((OUT_TM, k), lambda i: (i, 0)),
                  pl.BlockSpec((k, D_MODEL), lambda i: (0, 0)),
                  pl.BlockSpec((OUT_TM, D_MODEL), lambda i: (i, 0)), vec, vec],
        out_specs=[pl.BlockSpec((OUT_TM, D_MODEL), lambda i: (i, 0)),
                   pl.BlockSpec((OUT_TM, D_MODEL), lambda i: (i, 0))],
        out_shape=[jax.ShapeDtypeStruct((m, D_MODEL), F32), jax.ShapeDtypeStruct((m, D_MODEL), BF16)],
        compiler_params=_params("parallel"),
        name="proj_residual_ln",
    )(y, w, x, g, b)


FFN_TM = 1024
FFN_TN = 512
FFN_HALO = 16


def _ffn_up_kernel(x_ref, halo_ref, wg_ref, wv_ref, cwg_ref, cwv_ref, cbg_ref, cbv_ref, o_ref, xs_ref,
                   *, tiles_per_seq):
    @pl.when(pl.program_id(1) == 0)
    def _():
        first = (pl.program_id(0) % tiles_per_seq) == 0
        halo = halo_ref[...]
        xs_ref[0:FFN_HALO, :] = jnp.where(first, jnp.zeros_like(halo), halo)
        xs_ref[FFN_HALO:FFN_HALO + FFN_TM, :] = x_ref[...]

    xs = xs_ref[...]

    def conv(w_ref, cw_ref, cb_ref):
        r = _dot(xs, w_ref[...])
        y = cb_ref[...] + cw_ref[FFN_CONV - 1:FFN_CONV, :] * r[FFN_HALO:FFN_HALO + FFN_TM]
        for j in range(FFN_CONV - 1):
            off = FFN_HALO - (FFN_CONV - 1) + j
            y = y + cw_ref[j:j + 1, :] * r[off:off + FFN_TM]
        return y

    hg = conv(wg_ref, cwg_ref, cbg_ref)
    hv = conv(wv_ref, cwv_ref, cbv_ref)
    o_ref[...] = (hg * _sigmoid(hg) * hv).astype(o_ref.dtype)


def ffn_up(xb, seq_len, wg, wv, cwg, cwv, cbg, cbv):
    m, k = xb.shape
    n = wg.shape[1]
    assert seq_len % FFN_TM == 0
    halo_blocks = FFN_TM // FFN_HALO
    col = lambda depth: pl.BlockSpec((depth, FFN_TN), lambda i, j: (0, j))
    return pl.pallas_call(
        functools.partial(_ffn_up_kernel, tiles_per_seq=seq_len // FFN_TM),
        grid=(m // FFN_TM, n // FFN_TN),
        in_specs=[pl.BlockSpec((FFN_TM, k), lambda i, j: (i, 0)),
                  pl.BlockSpec((FFN_HALO, k), lambda i, j: (jnp.maximum(i * halo_blocks - 1, 0), 0)),
                  col(k), col(k), col(FFN_CONV), col(FFN_CONV), col(1), col(1)],
        out_specs=pl.BlockSpec((FFN_TM, FFN_TN), lambda i, j: (i, j)),
        out_shape=jax.ShapeDtypeStruct((m, n), BF16),
        scratch_shapes=[pltpu.VMEM((FFN_HALO + FFN_TM, k), BF16)],
        compiler_params=_params("parallel", "arbitrary"),
        name="ffn_up",
    )(xb, xb, wg, wv, cwg, cwv, cbg, cbv)


DOWN_TM = 512
DOWN_TK = 512


def _ffn_down_kernel(a_ref, w_ref, x_ref, g_ref, b_ref, o_ref, ob_ref, acc_ref):
    kk = pl.program_id(1)

    @pl.when(kk == 0)
    def _():
        acc_ref[...] = DEEPNORM_ALPHA * x_ref[...]

    acc_ref[...] += _dot(a_ref[...], w_ref[...])

    @pl.when(kk == pl.num_programs(1) - 1)
    def _():
        out = _layer_norm_rows(acc_ref[...], g_ref[...], b_ref[...])
        o_ref[...] = out
        ob_ref[...] = out.astype(BF16)


def ffn_down_ln(act, w, x, g, b):
    m, k = act.shape
    vec = pl.BlockSpec((1, D_MODEL), lambda i, j: (0, 0))
    row = pl.BlockSpec((DOWN_TM, D_MODEL), lambda i, j: (i, 0))
    return pl.pallas_call(
        _ffn_down_kernel,
        grid=(m // DOWN_TM, k // DOWN_TK),
        in_specs=[pl.BlockSpec((DOWN_TM, DOWN_TK), lambda i, j: (i, j)),
                  pl.BlockSpec((DOWN_TK, D_MODEL), lambda i, j: (j, 0)),
                  row, vec, vec],
        out_specs=[row, row],
        out_shape=[jax.ShapeDtypeStruct((m, D_MODEL), F32), jax.ShapeDtypeStruct((m, D_MODEL), BF16)],
        scratch_shapes=[pltpu.VMEM((DOWN_TM, D_MODEL), F32)],
        compiler_params=_params("parallel", "arbitrary"),
        name="ffn_down_ln",
    )(act, w, x, g, b)


_FOX_QKV = (0, 3 * WIDTH)
_FOX_F = (_FOX_QKV[1], _FOX_QKV[1] + N_HEADS)
_SGU_UV = (_FOX_F[1], _FOX_F[1] + 2 * WIDTH)
_GDN_QKV = (_SGU_UV[1], _SGU_UV[1] + 3 * WIDTH)
_GDN_A = (_GDN_QKV[1], _GDN_QKV[1] + N_HEADS)
_GDN_B = (_GDN_A[1], _GDN_A[1] + N_HEADS)
_GDN_GATE = (_GDN_B[1], _GDN_B[1] + WIDTH)
_GATES = (_GDN_GATE[1], _GDN_GATE[1] + 3 * D_MODEL)


def _cols(w, span):
    return w[..., span[0]:span[1]]


def _pad_lanes(v, offset):
    return jnp.zeros((1, AUX_LANES), F32).at[0, offset:offset + v.shape[0]].set(v)


def _layer(x, xb, bsz, seq, w_in, b_in, sgu_ln_g, sgu_ln_b, sgu_w, sgu_b, gdn_conv_w, gdn_a_log,
           gdn_dt_bias, gdn_norm_g, w_proj_a, w_proj_b, w_proj_c, w_out, ln1_g, ln1_b,
           ffn_w_up, ffn_conv_w, ffn_conv_b, ffn_w_down, ln2_g, ln2_b):
    tokens = bsz * seq
    wb = lambda span: _cols(w_in, span).astype(BF16)
    bias = lambda span: _cols(b_in, span)[None, :]

    fox_qkv = matmul_bias(xb, wb(_FOX_QKV), bias(_FOX_QKV), BF16, 1024, 512, "in_fox")
    sgu_uv = matmul_bias(xb, wb(_SGU_UV), bias(_SGU_UV), F32, 1024, 512, "in_sgu")
    gdn_qkv = matmul_bias(xb, wb(_GDN_QKV), bias(_GDN_QKV), F32, 1024, 512, "in_gdn")
    gdn_gate = matmul_bias(xb, wb(_GDN_GATE), bias(_GDN_GATE), F32, 1024, 512, "in_gdn_gate")
    pad = AUX_LANES - 3 * N_HEADS
    w_small = jnp.concatenate([_cols(w_in, _FOX_F), _cols(w_in, _GDN_A), _cols(w_in, _GDN_B),
                               jnp.zeros((D_MODEL, pad), F32)], axis=1).astype(BF16)
    b_small = jnp.concatenate([_cols(b_in, _FOX_F), _cols(b_in, _GDN_A), _cols(b_in, _GDN_B),
                               jnp.zeros((pad,), F32)])[None, :]
    z_small = matmul_bias(xb, w_small, b_small, F32, 1024, AUX_LANES, "in_small")

    aux = aux_scalars(z_small.reshape(bsz, seq, AUX_LANES), _pad_lanes(gdn_a_log, AUX_GC),
                      _pad_lanes(gdn_dt_bias, AUX_GC))
    head_rows = lambda off: jnp.swapaxes(aux[:, :, off:off + N_HEADS], 1, 2).reshape(bsz * N_HEADS, seq)
    c_row = head_rows(AUX_C).reshape(bsz * N_HEADS, seq // FOX_TK, FOX_TK)
    gc_row = head_rows(AUX_GC).reshape(bsz, N_HEADS, seq)

    y_a = fox_attention(fox_qkv.reshape(bsz, seq, 3 * WIDTH), aux, c_row)
    y_b = spatial_gating(sgu_uv.reshape(bsz, seq, 2 * WIDTH), sgu_ln_g[None, :], sgu_ln_b[None, :],
                         sgu_w, sgu_b.T)
    y_c = gdn_mixer(gdn_qkv.reshape(bsz, seq, 3 * WIDTH), gdn_conv_w, aux, gc_row,
                    gdn_gate.reshape(bsz, seq, WIDTH), gdn_norm_g[None, :])

    g0 = _GATES[0]
    wg = [w_in[:, g0 + i * D_MODEL:g0 + (i + 1) * D_MODEL].astype(BF16) for i in range(3)]
    bg = [b_in[None, g0 + i * D_MODEL:g0 + (i + 1) * D_MODEL] for i in range(3)]
    wp = [w_proj_a.astype(BF16), w_proj_b.astype(BF16), w_proj_c.astype(BF16)]
    merged = merge_branches(xb, y_a.reshape(tokens, WIDTH), y_b.reshape(tokens, WIDTH),
                            y_c.reshape(tokens, WIDTH), wg, bg, wp)
    x, xb = proj_residual_ln(merged, w_out.astype(BF16), x, ln1_g[None, :], ln1_b[None, :])

    ff_pad = D_FF_PAD - D_FF
    pad_cols = lambda t: jnp.pad(t, ((0, 0), (0, ff_pad)))
    act = ffn_up(xb, seq,
                 pad_cols(ffn_w_up[:, :D_FF]).astype(BF16), pad_cols(ffn_w_up[:, D_FF:]).astype(BF16),
                 pad_cols(ffn_conv_w[:, :D_FF]), pad_cols(ffn_conv_w[:, D_FF:]),
                 pad_cols(ffn_conv_b[None, :D_FF]), pad_cols(ffn_conv_b[None, D_FF:]))
    w_down = jnp.pad(ffn_w_down, ((0, ff_pad), (0, 0))).astype(BF16)
    return ffn_down_ln(act, w_down, x, ln2_g[None, :], ln2_b[None, :])


@jax.jit
def kernel(x, w_in, b_in, sgu_ln_g, sgu_ln_b, sgu_w, sgu_b, gdn_conv_w, gdn_a_log, gdn_dt_bias, gdn_norm_g, w_proj_a, w_proj_b, w_proj_c, w_out, ln1_g, ln1_b, ffn_w_up, ffn_conv_w, ffn_conv_b, ffn_w_down, ln2_g, ln2_b):
    bsz, seq, _ = x.shape
    stacked = (w_in, b_in, sgu_ln_g, sgu_ln_b, sgu_w, sgu_b, gdn_conv_w, gdn_a_log, gdn_dt_bias,
               gdn_norm_g, w_proj_a, w_proj_b, w_proj_c, w_out, ln1_g, ln1_b, ffn_w_up, ffn_conv_w,
               ffn_conv_b, ffn_w_down, ln2_g, ln2_b)
    x = x.reshape(bsz * seq, D_MODEL)
    xb = x.astype(BF16)
    for layer in range(w_in.shape[0]):
        x, xb = _layer(x, xb, bsz, seq, *(t[layer] for t in stacked))
    return x.reshape(bsz, seq, D_MODEL)
```

```python
import functools

import jax
import jax.numpy as jnp
from jax import lax
from jax.experimental import pallas as pl
from jax.experimental.pallas import tpu as pltpu

F32 = jnp.float32
BF16 = jnp.bfloat16

D_MODEL = 2048
HEAD_DIM = 128
N_HEADS = 8
WIDTH = N_HEADS * HEAD_DIM
CHUNK = 64
SGU_SPAN = 128
GDN_CONV = 4
D_FF = 5504
D_FF_PAD = 5632
FFN_CONV = 3
DEPTH = 4
DEEPNORM_ALPHA = (2 * DEPTH) ** 0.25
LN_EPS = 1e-5
RMS_EPS = 1e-6

AUX_LANES = 128
AUX_C = 0
AUX_GC = 8
AUX_BETA = 16

VMEM_LIMIT = 56 * 1024 * 1024

NEG_INF = float("-inf")


def _params(*semantics):
    return pltpu.CompilerParams(dimension_semantics=semantics, vmem_limit_bytes=VMEM_LIMIT)


def _dot(a, b):
    return jnp.dot(a, b, preferred_element_type=F32)


def _dot_nt(a, b):
    return lax.dot_general(a, b, (((1,), (1,)), ((), ())), preferred_element_type=F32)


def _dot_tn(a, b):
    return lax.dot_general(a, b, (((0,), (0,)), ((), ())), preferred_element_type=F32)


def _split2(x):
    hi = x.astype(BF16)
    lo = (x - hi.astype(F32)).astype(BF16)
    return hi, lo


def _split3(x):
    hi = x.astype(BF16)
    r1 = x - hi.astype(F32)
    mid = r1.astype(BF16)
    lo = (r1 - mid.astype(F32)).astype(BF16)
    return hi, mid, lo


def _mm3(a, b):
    ah, al = _split2(a)
    bh, bl = _split2(b)
    return _dot(ah, bh) + _dot(ah, bl) + _dot(al, bh)


def _softplus(z):
    return jnp.maximum(z, 0.0) + jnp.log1p(jnp.exp(-jnp.abs(z)))


def _sigmoid(z):
    return 1.0 / (1.0 + jnp.exp(-z))


def _layer_norm_rows(y, g, b):
    mu = jnp.mean(y, axis=-1, keepdims=True)
    yc = y - mu
    var = jnp.mean(yc * yc, axis=-1, keepdims=True)
    return yc * lax.rsqrt(var + LN_EPS) * g + b


def _mm_bias_kernel(x_ref, w_ref, b_ref, o_ref):
    acc = _dot(x_ref[...], w_ref[...])
    o_ref[...] = (acc + b_ref[...]).astype(o_ref.dtype)


def _mm_bias_scale_kernel(x_ref, w_ref, b_ref, s_ref, o_ref):
    acc = _dot(x_ref[...], w_ref[...])
    o_ref[...] = ((acc + b_ref[...]) * s_ref[...]).astype(o_ref.dtype)


def matmul_bias(xb, w, b, out_dtype, tm, tn, name, col_scale=None):
    m, k = xb.shape
    n = w.shape[1]
    vec = pl.BlockSpec((1, tn), lambda i, j: (0, j))
    extra = () if col_scale is None else (col_scale,)
    return pl.pallas_call(
        _mm_bias_kernel if col_scale is None else _mm_bias_scale_kernel,
        grid=(m // tm, n // tn),
        in_specs=[pl.BlockSpec((tm, k), lambda i, j: (i, 0)),
                  pl.BlockSpec((k, tn), lambda i, j: (0, j)), vec] + [vec] * len(extra),
        out_specs=pl.BlockSpec((tm, tn), lambda i, j: (i, j)),
        out_shape=jax.ShapeDtypeStruct((m, n), out_dtype),
        compiler_params=_params("parallel", "arbitrary"),
        name=name,
    )(xb, w, b, *extra)


AUX_BLOCK = 256


def _aux_kernel(z_ref, alog_ref, dt_ref, o_ref, carry_ref):
    @pl.when(pl.program_id(1) == 0)
    def _():
        carry_ref[...] = jnp.zeros_like(carry_ref)

    z = z_ref[0]
    lane = lax.broadcasted_iota(jnp.int32, z.shape, 1)
    log_f = -_softplus(-z)
    g = -jnp.exp(alog_ref[...]) * _softplus(z + dt_ref[...])
    beta = _sigmoid(z)

    ri = lax.broadcasted_iota(jnp.int32, (AUX_BLOCK, AUX_BLOCK), 0)
    ci = lax.broadcasted_iota(jnp.int32, (AUX_BLOCK, AUX_BLOCK), 1)
    tri = ri >= ci
    ones_seq = jnp.where(tri, 1.0, 0.0).astype(BF16)
    ones_chunk = jnp.where(tri & ((ri // CHUNK) == (ci // CHUNK)), 1.0, 0.0).astype(BF16)

    def cumsum(ones, val):
        hi, mid, lo = _split3(val)
        return _dot(ones, hi) + _dot(ones, mid) + _dot(ones, lo)

    c = cumsum(ones_seq, log_f) + carry_ref[...]
    carry_ref[...] = c[AUX_BLOCK - 1:AUX_BLOCK, :]
    gc = cumsum(ones_chunk, g)
    o_ref[0] = jnp.where(lane < AUX_GC, c, jnp.where(lane < AUX_BETA, gc, beta))


def aux_scalars(z, alog_pad, dt_pad):
    b, s, _ = z.shape
    return pl.pallas_call(
        _aux_kernel,
        grid=(b, s // AUX_BLOCK),
        in_specs=[pl.BlockSpec((1, AUX_BLOCK, AUX_LANES), lambda i, j: (i, j, 0)),
                  pl.BlockSpec((1, AUX_LANES), lambda i, j: (0, 0)),
                  pl.BlockSpec((1, AUX_LANES), lambda i, j: (0, 0))],
        out_specs=pl.BlockSpec((1, AUX_BLOCK, AUX_LANES), lambda i, j: (i, j, 0)),
        out_shape=jax.ShapeDtypeStruct((b, s, AUX_LANES), F32),
        scratch_shapes=[pltpu.VMEM((1, AUX_LANES), F32)],
        compiler_params=_params("parallel", "arbitrary"),
        name="aux_scalars",
    )(z, alog_pad, dt_pad)


def _pick_lane(x, idx):
    lane = lax.broadcasted_iota(jnp.int32, x.shape, 1)
    return jnp.sum(jnp.where(lane == idx, x, 0.0), axis=1, keepdims=True)


LOG2E = 1.4426950408889634
FOX_T = 512
FOX_QSCALE = LOG2E * HEAD_DIM ** -0.5
FOX_BIAS_TERMS = 3
FOX_GROUP = 4


def _fox_kernel(qt_ref, k_ref, vt_ref, aux_ref, crow_ref, o_ref, kaug_ref):
    hg = pl.program_id(1)
    qi = pl.program_id(2)
    nk = kaug_ref.shape[1] // FOX_T
    group = range(FOX_GROUP)

    @pl.when(qi == 0)
    def _():
        lane = lax.broadcasted_iota(jnp.int32, (FOX_T, HEAD_DIM), 1)
        for g in group:
            for j in range(nk):
                rows = slice(j * FOX_T, (j + 1) * FOX_T)
                ck = -LOG2E * _pick_lane(aux_ref[0, rows, :], AUX_C + hg * FOX_GROUP + g)
                pieces = [t.astype(F32) for t in _split3(ck)]
                bias = jnp.zeros((FOX_T, HEAD_DIM), F32)
                for i in reversed(range(FOX_BIAS_TERMS)):
                    bias = jnp.where(lane == i, pieces[i], bias)
                kaug_ref[g, rows, 0:HEAD_DIM] = k_ref[0, rows, g * HEAD_DIM:(g + 1) * HEAD_DIM]
                kaug_ref[g, rows, HEAD_DIM:2 * HEAD_DIM] = bias.astype(BF16)

    sub = lax.broadcasted_iota(jnp.int32, (HEAD_DIM, FOX_T), 0)
    ones = jnp.where(sub < FOX_BIAS_TERMS, 1.0, 0.0).astype(BF16)
    q_aug = [jnp.concatenate([qt_ref[0, g, 0], ones], axis=0) for g in group]
    cq = [LOG2E * crow_ref[g, pl.ds(qi, 1), :] for g in group]

    def block(j, carry, diagonal):
        start = pl.multiple_of(j * FOX_T, FOX_T)
        scores = [_dot(kaug_ref[g, pl.ds(start, FOX_T), :], q_aug[g]) for g in group]
        stats, probs = [], []
        for g in group:
            m, l, _ = carry[g]
            s = scores[g]
            if diagonal:
                key = lax.broadcasted_iota(jnp.int32, s.shape, 0)
                qry = lax.broadcasted_iota(jnp.int32, s.shape, 1)
                s = jnp.where(qry >= key, s, NEG_INF)
            m_new = jnp.maximum(m, jnp.max(s, axis=0, keepdims=True) + cq[g])
            alpha = jnp.exp2(m - m_new)
            p = jnp.exp2(s + (cq[g] - m_new))
            stats.append((m_new, alpha * l + jnp.sum(p, axis=0, keepdims=True), alpha))
            probs.append(p.astype(BF16))
        out = []
        for g in group:
            m_new, l, alpha = stats[g]
            acc = alpha * carry[g][2] + _dot(vt_ref[0, g, j], probs[g])
            out.append((m_new, l, acc))
        return tuple(out)

    init = tuple((jnp.full((1, FOX_T), NEG_INF, F32), jnp.zeros((1, FOX_T), F32),
                  jnp.zeros((HEAD_DIM, FOX_T), F32)) for _ in group)
    carry = lax.fori_loop(0, qi, lambda j, c: block(j, c, False), init)
    carry = block(qi, carry, True)
    for g in group:
        _, l, acc = carry[g]
        o_ref[0, :, g * HEAD_DIM:(g + 1) * HEAD_DIM] = (acc / l).T.astype(o_ref.dtype)


def fox_attention(qkv, aux, c_row):
    b, s, _ = qkv.shape
    nt = s // FOX_T
    groups = N_HEADS // FOX_GROUP

    def head_major_t(cols):
        t = qkv[:, :, cols * WIDTH:(cols + 1) * WIDTH].reshape(b, nt, FOX_T, N_HEADS, HEAD_DIM)
        return jnp.transpose(t, (0, 3, 1, 4, 2))

    return pl.pallas_call(
        _fox_kernel,
        grid=(b, groups, nt),
        in_specs=[pl.BlockSpec((1, FOX_GROUP, 1, HEAD_DIM, FOX_T), lambda i, h, t: (i, h, t, 0, 0)),
                  pl.BlockSpec((1, s, FOX_GROUP * HEAD_DIM), lambda i, h, t: (i, 0, groups + h)),
                  pl.BlockSpec((1, FOX_GROUP, nt, HEAD_DIM, FOX_T), lambda i, h, t: (i, h, 0, 0, 0)),
                  pl.BlockSpec((1, s, AUX_LANES), lambda i, h, t: (i, 0, 0)),
                  pl.BlockSpec((FOX_GROUP, nt, FOX_T), lambda i, h, t: (i * groups + h, 0, 0))],
        out_specs=pl.BlockSpec((1, FOX_T, FOX_GROUP * HEAD_DIM), lambda i, h, t: (i, t, h)),
        out_shape=jax.ShapeDtypeStruct((b, s, WIDTH), BF16),
        scratch_shapes=[pltpu.VMEM((FOX_GROUP, s, 2 * HEAD_DIM), BF16)],
        compiler_params=_params("parallel", "parallel", "arbitrary"),
        name="fox_attention",
    )(head_major_t(0), qkv, head_major_t(2), aux, c_row)


def _sgu_kernel(u_ref, v_ref, g_ref, b_ref, w_ref, bst_ref, o_ref):
    ri = lax.broadcasted_iota(jnp.int32, (SGU_SPAN, SGU_SPAN), 0)
    ci = lax.broadcasted_iota(jnp.int32, (SGU_SPAN, SGU_SPAN), 1)
    mask = (ri // CHUNK) >= (ci // CHUNK)
    for g in range(N_HEADS):
        cols = slice(g * HEAD_DIM, (g + 1) * HEAD_DIM)
        vn = _layer_norm_rows(v_ref[0, :, cols], g_ref[:, cols], b_ref[:, cols])
        wm = jnp.where(mask, w_ref[g], 0.0).astype(BF16)
        mixed = _dot(wm, vn.astype(BF16)) + bst_ref[:, g:g + 1]
        o_ref[0, :, cols] = (u_ref[0, :, cols] * mixed).astype(o_ref.dtype)


def spatial_gating(uv, ln_g, ln_b, w_s, b_s_t):
    b, s, _ = uv.shape
    return pl.pallas_call(
        _sgu_kernel,
        grid=(b, s // SGU_SPAN),
        in_specs=[pl.BlockSpec((1, SGU_SPAN, WIDTH), lambda i, j: (i, j, 0)),
                  pl.BlockSpec((1, SGU_SPAN, WIDTH), lambda i, j: (i, j, 1)),
                  pl.BlockSpec((1, WIDTH), lambda i, j: (0, 0)),
                  pl.BlockSpec((1, WIDTH), lambda i, j: (0, 0)),
                  pl.BlockSpec((N_HEADS, SGU_SPAN, SGU_SPAN), lambda i, j: (0, 0, 0)),
                  pl.BlockSpec((SGU_SPAN, N_HEADS), lambda i, j: (0, 0))],
        out_specs=pl.BlockSpec((1, SGU_SPAN, WIDTH), lambda i, j: (i, j, 0)),
        out_shape=jax.ShapeDtypeStruct((b, s, WIDTH), BF16),
        compiler_params=_params("parallel", "parallel"),
        name="spatial_gating",
    )(uv, uv, ln_g, ln_b, w_s, b_s_t)


GDN_PAIR = 2 * CHUNK
GDN_HALO = 8


def _gdn_prep_kernel(x_ref, halo_ref, cw_ref, aux_ref, gcrow_ref, u_ref, w_ref, qd_ref, kdt_ref, qk_ref,
                     pad_ref):
    first = pl.program_id(1) == 0
    halo = halo_ref[0]
    pad_ref[0:GDN_HALO, :] = jnp.where(first, jnp.zeros_like(halo), halo)
    pad_ref[GDN_HALO:GDN_HALO + GDN_PAIR, :] = x_ref[0]

    def conv_silu(cols):
        y = cw_ref[GDN_CONV - 1:GDN_CONV, cols] * pad_ref[GDN_HALO:GDN_HALO + GDN_PAIR, cols]
        for j in range(GDN_CONV - 1):
            off = GDN_HALO - (GDN_CONV - 1) + j
            y = y + cw_ref[j:j + 1, cols] * pad_ref[off:off + GDN_PAIR, cols]
        return y * _sigmoid(y)

    def l2n(t):
        return t * lax.rsqrt(jnp.sum(t * t, axis=-1, keepdims=True) + RMS_EPS)

    ri = lax.broadcasted_iota(jnp.int32, (GDN_PAIR, GDN_PAIR), 0)
    ci = lax.broadcasted_iota(jnp.int32, (GDN_PAIR, GDN_PAIR), 1)
    same_chunk = (ri // CHUNK) == (ci // CHUNK)
    causal = same_chunk & (ri >= ci)
    strict = same_chunk & (ri > ci)
    row = lax.broadcasted_iota(jnp.int32, (GDN_PAIR, 1), 0)
    aux = aux_ref[0]

    heads = range(N_HEADS)
    head_cols = [slice(h * HEAD_DIM, (h + 1) * HEAD_DIM) for h in heads]
    ys, ps, rhss = [], [], []
    for h in heads:
        cols = head_cols[h]
        q = l2n(conv_silu(slice(h * HEAD_DIM, (h + 1) * HEAD_DIM))) * (HEAD_DIM ** -0.5)
        k = l2n(conv_silu(slice(WIDTH + h * HEAD_DIM, WIDTH + (h + 1) * HEAD_DIM)))
        v = conv_silu(slice(2 * WIDTH + h * HEAD_DIM, 2 * WIDTH + (h + 1) * HEAD_DIM))
        gc = aux[:, AUX_GC + h:AUX_GC + h + 1]
        beta = aux[:, AUX_BETA + h:AUX_BETA + h + 1]
        gc_r = gcrow_ref[0, h:h + 1, :]
        decay = jnp.exp(jnp.where(causal, gc - gc_r, NEG_INF))
        kb = k * beta
        kbf = k.astype(BF16)
        prod = _dot_nt(jnp.concatenate([kb, q], axis=0).astype(BF16), kbf)
        a = jnp.where(strict, prod[:GDN_PAIR] * decay, 0.0)
        qk_ref[0, :, cols] = (prod[GDN_PAIR:] * decay).astype(BF16)
        egc = jnp.exp(gc)
        rhss.append(jnp.concatenate([v * beta, kb * egc], axis=1))
        qd_ref[0, :, cols] = (q * egc).astype(BF16)
        g_last = jnp.where(row < CHUNK, gc[CHUNK - 1:CHUNK, :], gc[GDN_PAIR - 1:GDN_PAIR, :])
        kdt_ref[0, cols, :] = (k * jnp.exp(g_last - gc)).T.astype(BF16)
        ab = a.astype(BF16)
        ys.append(-a)
        ps.append(_dot(ab, ab))

    n = 2
    while True:
        pbs = [p.astype(BF16) for p in ps]
        ys = [ys[h] + ps[h] + _dot(ys[h].astype(BF16), pbs[h]) for h in heads]
        n *= 2
        if n >= CHUNK:
            break
        ps = [_dot(pb, pb) for pb in pbs]

    for h in heads:
        sol = rhss[h] + _dot(ys[h].astype(BF16), rhss[h].astype(BF16))
        u_ref[0, :, head_cols[h]] = sol[:, :HEAD_DIM]
        w_ref[0, :, head_cols[h]] = sol[:, HEAD_DIM:].astype(BF16)


def gdn_prepare(qkv, conv_w, aux, gc_row):
    b, s, _ = qkv.shape
    halo_blocks = GDN_PAIR // GDN_HALO
    tok = pl.BlockSpec((1, GDN_PAIR, WIDTH), lambda i, t: (i, t, 0))
    wide = lambda dt: jax.ShapeDtypeStruct((b, s, WIDTH), dt)
    return pl.pallas_call(
        _gdn_prep_kernel,
        grid=(b, s // GDN_PAIR),
        in_specs=[pl.BlockSpec((1, GDN_PAIR, 3 * WIDTH), lambda i, t: (i, t, 0)),
                  pl.BlockSpec((1, GDN_HALO, 3 * WIDTH),
                               lambda i, t: (i, jnp.maximum(t * halo_blocks - 1, 0), 0)),
                  pl.BlockSpec((GDN_CONV, 3 * WIDTH), lambda i, t: (0, 0)),
                  pl.BlockSpec((1, GDN_PAIR, AUX_LANES), lambda i, t: (i, t, 0)),
                  pl.BlockSpec((1, N_HEADS, GDN_PAIR), lambda i, t: (i, 0, t))],
        out_specs=[tok, tok, tok, pl.BlockSpec((1, WIDTH, GDN_PAIR), lambda i, t: (i, 0, t)), tok],
        out_shape=[wide(F32), wide(BF16), wide(BF16), jax.ShapeDtypeStruct((b, WIDTH, s), BF16), wide(BF16)],
        scratch_shapes=[pltpu.VMEM((GDN_HALO + GDN_PAIR, 3 * WIDTH), F32)],
        compiler_params=_params("parallel", "parallel"),
        name="gdn_prepare",
    )(qkv, qkv, conv_w, aux, gc_row)


SCAN_BLOCK = 256


def _gdn_scan_kernel(u_ref, w_ref, qd_ref, qk_ref, kdt_ref, aux_ref, gate_ref, ng_ref, o_ref, state_ref):
    @pl.when(pl.program_id(1) == 0)
    def _():
        state_ref[...] = jnp.zeros_like(state_ref)

    zeros = jnp.zeros((CHUNK, HEAD_DIM), BF16)
    for c in range(SCAN_BLOCK // CHUNK):
        rows = slice(c * CHUNK, (c + 1) * CHUNK)
        pair = slice((c // 2) * GDN_PAIR, (c // 2 + 1) * GDN_PAIR)
        last = (c + 1) * CHUNK - 1
        for h in range(N_HEADS):
            cols = slice(h * HEAD_DIM, (h + 1) * HEAD_DIM)
            st = state_ref[h]
            r1 = _dot(jnp.concatenate([w_ref[0, rows, cols], qd_ref[0, rows, cols]], axis=0),
                      st.astype(BF16))
            v_new = (u_ref[0, rows, cols] - r1[:CHUNK]).astype(BF16)
            v_pad = jnp.concatenate([v_new, zeros] if c % 2 == 0 else [zeros, v_new], axis=0)
            r2 = _dot(jnp.concatenate([qk_ref[0, rows, cols], kdt_ref[0, cols, pair]], axis=0), v_pad)
            decay_last = jnp.exp(aux_ref[0, last:last + 1, AUX_GC + h:AUX_GC + h + 1])
            state_ref[h] = st * decay_last + r2[CHUNK:]
            o = r1[CHUNK:] + r2[:CHUNK]
            o = o * lax.rsqrt(jnp.mean(o * o, axis=-1, keepdims=True) + RMS_EPS) * ng_ref[...]
            gt = gate_ref[0, rows, cols]
            o_ref[0, rows, cols] = (o * (gt * _sigmoid(gt))).astype(o_ref.dtype)


def gdn_scan(u, w, qd, qk, kdt, aux, gate, norm_g):
    b, s, _ = u.shape
    tok = pl.BlockSpec((1, SCAN_BLOCK, WIDTH), lambda i, t: (i, t, 0))
    return pl.pallas_call(
        _gdn_scan_kernel,
        grid=(b, s // SCAN_BLOCK),
        in_specs=[tok, tok, tok, tok,
                  pl.BlockSpec((1, WIDTH, SCAN_BLOCK), lambda i, t: (i, 0, t)),
                  pl.BlockSpec((1, SCAN_BLOCK, AUX_LANES), lambda i, t: (i, t, 0)),
                  tok, pl.BlockSpec((1, HEAD_DIM), lambda i, t: (0, 0))],
        out_specs=tok,
        out_shape=jax.ShapeDtypeStruct((b, s, WIDTH), BF16),
        scratch_shapes=[pltpu.VMEM((N_HEADS, HEAD_DIM, HEAD_DIM), F32)],
        compiler_params=_params("parallel", "arbitrary"),
        name="gdn_scan",
    )(u, w, qd, qk, kdt, aux, gate, norm_g)


def gdn_mixer(qkv, conv_w, aux, gc_row, gate, norm_g):
    u, w, qd, kdt, qk = gdn_prepare(qkv, conv_w, aux, gc_row)
    return gdn_scan(u, w, qd, qk, kdt, aux, gate, norm_g)


MERGE_TM = 512
MERGE_TN = 512


def _merge_kernel(xb_ref, ya_ref, yb_ref, yc_ref, wg0, wg1, wg2, bg0, bg1, bg2, wa, wb, wc, o_ref):
    xb = xb_ref[...]

    def branch(y_ref, wp, wg, bg):
        return _sigmoid(_dot(xb, wg[...]) + bg[...]) * _dot(y_ref[...], wp[...])

    merged = branch(ya_ref, wa, wg0, bg0) + branch(yb_ref, wb, wg1, bg1) + branch(yc_ref, wc, wg2, bg2)
    o_ref[...] = merged.astype(o_ref.dtype)


def merge_branches(xb, ya, yb, yc, wg, bg, wp):
    m = xb.shape[0]
    row = lambda width: pl.BlockSpec((MERGE_TM, width), lambda i, j: (i, 0))
    col = lambda depth: pl.BlockSpec((depth, MERGE_TN), lambda i, j: (0, j))
    return pl.pallas_call(
        _merge_kernel,
        grid=(m // MERGE_TM, D_MODEL // MERGE_TN),
        in_specs=[row(D_MODEL), row(WIDTH), row(WIDTH), row(WIDTH),
                  col(D_MODEL), col(D_MODEL), col(D_MODEL), col(1), col(1), col(1),
                  col(WIDTH), col(WIDTH), col(WIDTH)],
        out_specs=pl.BlockSpec((MERGE_TM, MERGE_TN), lambda i, j: (i, j)),
        out_shape=jax.ShapeDtypeStruct((m, D_MODEL), BF16),
        compiler_params=_params("parallel", "arbitrary"),
        name="merge_branches",
    )(xb, ya, yb, yc, *wg, *bg, *wp)


OUT_TM = 256


def _proj_ln_kernel(y_ref, w_ref, x_ref, g_ref, b_ref, o_ref, ob_ref):
    r = DEEPNORM_ALPHA * x_ref[...] + _dot(y_ref[...], w_ref[...])
    out = _layer_norm_rows(r, g_ref[...], b_ref[...])
    o_ref[...] = out
    ob_ref[...] = out.astype(BF16)


def proj_residual_ln(y, w, x, g, b):
    m, k = y.shape
    vec = pl.BlockSpec((1, D_MODEL), lambda i: (0, 0))
    row = pl.BlockSpec((OUT_TM, D_MODEL), lambda i: (i, 0))
    return pl.pallas_call(
        _proj_ln_kernel,
        grid=(m // OUT_TM,),
        in_specs=[pl.BlockSpec((OUT_TM, k), lambda i: (i, 0)),
                  pl.BlockSpec((k, D_MODEL), lambda i: (0, 0)),
                  row, vec, vec],
        out_specs=[row, row],
        out_shape=[jax.ShapeDtypeStruct((m, D_MODEL), F32), jax.ShapeDtypeStruct((m, D_MODEL), BF16)],
        compiler_params=_params("parallel"),
        name="proj_residual_ln",
    )(y, w, x, g, b)


DOWN_TM = 512
DOWN_TK = D_FF_PAD // 4


def _ffn_down_kernel(a_ref, w_ref, x_ref, g_ref, b_ref, o_ref, ob_ref, acc_ref):
    kk = pl.program_id(1)

    @pl.when(kk == 0)
    def _():
        acc_ref[...] = DEEPNORM_ALPHA * x_ref[...]

    acc_ref[...] += _dot(a_ref[...], w_ref[...])

    @pl.when(kk == pl.num_programs(1) - 1)
    def _():
        out = _layer_norm_rows(acc_ref[...], g_ref[...], b_ref[...])
        o_ref[...] = out
        ob_ref[...] = out.astype(BF16)


def ffn_down_ln(act, w, x, g, b):
    m, k = act.shape
    vec = pl.BlockSpec((1, D_MODEL), lambda i, j: (0, 0))
    row = pl.BlockSpec((DOWN_TM, D_MODEL), lambda i, j: (i, 0))
    return pl.pallas_call(
        _ffn_down_kernel,
        grid=(m // DOWN_TM, k // DOWN_TK),
        in_specs=[pl.BlockSpec((DOWN_TM, DOWN_TK), lambda i, j: (i, j)),
                  pl.BlockSpec((DOWN_TK, D_MODEL), lambda i, j: (j, 0)),
                  row, vec, vec],
        out_specs=[row, row],
        out_shape=[jax.ShapeDtypeStruct((m, D_MODEL), F32), jax.ShapeDtypeStruct((m, D_MODEL), BF16)],
        scratch_shapes=[pltpu.VMEM((DOWN_TM, D_MODEL), F32)],
        compiler_params=_params("parallel", "arbitrary"),
        name="ffn_down_ln",
    )(act, w, x, g, b)


FFN_TM = 1024
FFN_TN = 512
FFN_HALO = 16


def _ffn_up_kernel(x_ref, halo_ref, wg_ref, wv_ref, cwg_ref, cwv_ref, cbg_ref, cbv_ref, o_ref, xs_ref,
                   *, tiles_per_seq):
    @pl.when(pl.program_id(1) == 0)
    def _():
        first = (pl.program_id(0) % tiles_per_seq) == 0
        halo = halo_ref[...]
        xs_ref[0:FFN_HALO, :] = jnp.where(first, jnp.zeros_like(halo), halo)
        xs_ref[FFN_HALO:FFN_HALO + FFN_TM, :] = x_ref[...]

    xs = xs_ref[...]

    def conv(w_ref, cw_ref, cb_ref):
        r = _dot(xs, w_ref[...])
        y = cb_ref[...] + cw_ref[FFN_CONV - 1:FFN_CONV, :] * r[FFN_HALO:FFN_HALO + FFN_TM]
        for j in range(FFN_CONV - 1):
            off = FFN_HALO - (FFN_CONV - 1) + j
            y = y + cw_ref[j:j + 1, :] * r[off:off + FFN_TM]
        return y

    hg = conv(wg_ref, cwg_ref, cbg_ref)
    hv = conv(wv_ref, cwv_ref, cbv_ref)
    o_ref[...] = (hg * _sigmoid(hg) * hv).astype(o_ref.dtype)


def ffn_up(xb, seq_len, wg, wv, cwg, cwv, cbg, cbv):
    m, k = xb.shape
    n = wg.shape[1]
    assert seq_len % FFN_TM == 0
    halo_blocks = FFN_TM // FFN_HALO
    col = lambda depth: pl.BlockSpec((depth, FFN_TN), lambda i, j: (0, j))
    return pl.pallas_call(
        functools.partial(_ffn_up_kernel, tiles_per_seq=seq_len // FFN_TM),
        grid=(m // FFN_TM, n // FFN_TN),
        in_specs=[pl.BlockSpec((FFN_TM, k), lambda i, j: (i, 0)),
                  pl.BlockSpec((FFN_HALO, k), lambda i, j: (jnp.maximum(i * halo_blocks - 1, 0), 0)),
                  col(k), col(k), col(FFN_CONV), col(FFN_CONV), col(1), col(1)],
        out_specs=pl.BlockSpec((FFN_TM, FFN_TN), lambda i, j: (i, j)),
        out_shape=jax.ShapeDtypeStruct((m, n), BF16),
        scratch_shapes=[pltpu.VMEM((FFN_HALO + FFN_TM, k), BF16)],
        compiler_params=_params("parallel", "arbitrary"),
        name="ffn_up",
    )(xb, xb, wg, wv, cwg, cwv, cbg, cbv)


_FOX_QKV = (0, 3 * WIDTH)
_FOX_F = (_FOX_QKV[1], _FOX_QKV[1] + N_HEADS)
_SGU_UV = (_FOX_F[1], _FOX_F[1] + 2 * WIDTH)
_GDN_QKV = (_SGU_UV[1], _SGU_UV[1] + 3 * WIDTH)
_GDN_A = (_GDN_QKV[1], _GDN_QKV[1] + N_HEADS)
_GDN_B = (_GDN_A[1], _GDN_A[1] + N_HEADS)
_GDN_GATE = (_GDN_B[1], _GDN_B[1] + WIDTH)
_GATES = (_GDN_GATE[1], _GDN_GATE[1] + 3 * D_MODEL)


def _cols(w, span):
    return w[..., span[0]:span[1]]


def _pad_lanes(v, offset):
    return jnp.zeros((1, AUX_LANES), F32).at[0, offset:offset + v.shape[0]].set(v)


def _layer(x, xb, bsz, seq, w_in, b_in, sgu_ln_g, sgu_ln_b, sgu_w, sgu_b, gdn_conv_w, gdn_a_log,
           gdn_dt_bias, gdn_norm_g, w_proj_a, w_proj_b, w_proj_c, w_out, ln1_g, ln1_b,
           ffn_w_up, ffn_conv_w, ffn_conv_b, ffn_w_down, ln2_g, ln2_b):
    tokens = bsz * seq
    wb = lambda span: _cols(w_in, span).astype(BF16)
    bias = lambda span: _cols(b_in, span)[None, :]

    q_scale = jnp.concatenate([jnp.full((1, WIDTH), FOX_QSCALE, F32), jnp.ones((1, 2 * WIDTH), F32)], axis=1)
    fox_qkv = matmul_bias(xb, wb(_FOX_QKV), bias(_FOX_QKV), BF16, 1024, 1024, "in_fox", col_scale=q_scale)
    sgu_uv = matmul_bias(xb, wb(_SGU_UV), bias(_SGU_UV), F32, 1024, 1024, "in_sgu")
    gdn_qkv = matmul_bias(xb, wb(_GDN_QKV), bias(_GDN_QKV), F32, 1024, 1024, "in_gdn")
    gdn_gate = matmul_bias(xb, wb(_GDN_GATE), bias(_GDN_GATE), F32, 1024, 1024, "in_gdn_gate")
    pad = AUX_LANES - 3 * N_HEADS
    w_small = jnp.concatenate([_cols(w_in, _FOX_F), _cols(w_in, _GDN_A), _cols(w_in, _GDN_B),
                               jnp.zeros((D_MODEL, pad), F32)], axis=1).astype(BF16)
    b_small = jnp.concatenate([_cols(b_in, _FOX_F), _cols(b_in, _GDN_A), _cols(b_in, _GDN_B),
                               jnp.zeros((pad,), F32)])[None, :]
    z_small = matmul_bias(xb, w_small, b_small, F32, 1024, AUX_LANES, "in_small")

    aux = aux_scalars(z_small.reshape(bsz, seq, AUX_LANES), _pad_lanes(gdn_a_log, AUX_GC),
                      _pad_lanes(gdn_dt_bias, AUX_GC))
    head_rows = lambda off: jnp.swapaxes(aux[:, :, off:off + N_HEADS], 1, 2).reshape(bsz * N_HEADS, seq)
    c_row = head_rows(AUX_C).reshape(bsz * N_HEADS, seq // FOX_T, FOX_T)
    gc_row = head_rows(AUX_GC).reshape(bsz, N_HEADS, seq)

    y_a = fox_attention(fox_qkv.reshape(bsz, seq, 3 * WIDTH), aux, c_row)
    y_b = spatial_gating(sgu_uv.reshape(bsz, seq, 2 * WIDTH), sgu_ln_g[None, :], sgu_ln_b[None, :],
                         sgu_w, sgu_b.T)
    y_c = gdn_mixer(gdn_qkv.reshape(bsz, seq, 3 * WIDTH), gdn_conv_w, aux, gc_row,
                    gdn_gate.reshape(bsz, seq, WIDTH), gdn_norm_g[None, :])

    g0 = _GATES[0]
    wg = [w_in[:, g0 + i * D_MODEL:g0 + (i + 1) * D_MODEL].astype(BF16) for i in range(3)]
    bg = [b_in[None, g0 + i * D_MODEL:g0 + (i + 1) * D_MODEL] for i in range(3)]
    wp = [w_proj_a.astype(BF16), w_proj_b.astype(BF16), w_proj_c.astype(BF16)]
    merged = merge_branches(xb, y_a.reshape(tokens, WIDTH), y_b.reshape(tokens, WIDTH),
                            y_c.reshape(tokens, WIDTH), wg, bg, wp)
    x, xb = proj_residual_ln(merged, w_out.astype(BF16), x, ln1_g[None, :], ln1_b[None, :])

    ff_pad = D_FF_PAD - D_FF
    pad_cols = lambda t: jnp.pad(t, ((0, 0), (0, ff_pad)))
    act = ffn_up(xb, seq,
                 pad_cols(ffn_w_up[:, :D_FF]).astype(BF16), pad_cols(ffn_w_up[:, D_FF:]).astype(BF16),
                 pad_cols(ffn_conv_w[:, :D_FF]), pad_cols(ffn_conv_w[:, D_FF:]),
                 pad_cols(ffn_conv_b[None, :D_FF]), pad_cols(ffn_conv_b[None, D_FF:]))
    w_down = jnp.pad(ffn_w_down, ((0, ff_pad), (0, 0))).astype(BF16)
    return ffn_down_ln(act, w_down, x, ln2_g[None, :], ln2_b[None, :])


@jax.jit
def kernel(x, w_in, b_in, sgu_ln_g, sgu_ln_b, sgu_w, sgu_b, gdn_conv_w, gdn_a_log, gdn_dt_bias, gdn_norm_g, w_proj_a, w_proj_b, w_proj_c, w_out, ln1_g, ln1_b, ffn_w_up, ffn_conv_w, ffn_conv_b, ffn_w_down, ln2_g, ln2_b):
    bsz, seq, _ = x.shape
    stacked = (w_in, b_in, sgu_ln_g, sgu_ln_b, sgu_w, sgu_b, gdn_conv_w, gdn_a_log, gdn_dt_bias,
               gdn_norm_g, w_proj_a, w_proj_b, w_proj_c, w_out, ln1_g, ln1_b, ffn_w_up, ffn_conv_w,
               ffn_conv_b, ffn_w_down, ln2_g, ln2_b)
    x = x.reshape(bsz * seq, D_MODEL)
    xb = x.astype(BF16)
    for layer in range(w_in.shape[0]):
        x, xb = _layer(x, xb, bsz, seq, *(t[layer] for t in stacked))
    return x.reshape(bsz, seq, D_MODEL)
```

```python
import functools

import jax
import jax.numpy as jnp
from jax import lax
from jax.experimental import pallas as pl
from jax.experimental.pallas import tpu as pltpu

F32 = jnp.float32
BF16 = jnp.bfloat16

D_MODEL = 2048
HEAD_DIM = 128
N_HEADS = 8
WIDTH = N_HEADS * HEAD_DIM
CHUNK = 64
SGU_SPAN = 128
GDN_CONV = 4
D_FF = 5504
D_FF_PAD = 5632
FFN_CONV = 3
DEPTH = 4
DEEPNORM_ALPHA = (2 * DEPTH) ** 0.25
LN_EPS = 1e-5
RMS_EPS = 1e-6

AUX_LANES = 128
AUX_C = 0
AUX_GC = 8
AUX_BETA = 16

VMEM_LIMIT = 56 * 1024 * 1024

NEG_INF = float("-inf")


def _params(*semantics):
    return pltpu.CompilerParams(dimension_semantics=semantics, vmem_limit_bytes=VMEM_LIMIT)


def _dot(a, b):
    return jnp.dot(a, b, preferred_element_type=F32)


def _dot_nt(a, b):
    return lax.dot_general(a, b, (((1,), (1,)), ((), ())), preferred_element_type=F32)


def _dot_tn(a, b):
    return lax.dot_general(a, b, (((0,), (0,)), ((), ())), preferred_element_type=F32)


def _split2(x):
    hi = x.astype(BF16)
    lo = (x - hi.astype(F32)).astype(BF16)
    return hi, lo


def _split3(x):
    hi = x.astype(BF16)
    r1 = x - hi.astype(F32)
    mid = r1.astype(BF16)
    lo = (r1 - mid.astype(F32)).astype(BF16)
    return hi, mid, lo


def _mm3(a, b):
    ah, al = _split2(a)
    bh, bl = _split2(b)
    return _dot(ah, bh) + _dot(ah, bl) + _dot(al, bh)


def _softplus(z):
    return jnp.maximum(z, 0.0) + jnp.log1p(jnp.exp(-jnp.abs(z)))


def _sigmoid(z):
    return 1.0 / (1.0 + jnp.exp(-z))


def _layer_norm_rows(y, g, b):
    mu = jnp.mean(y, axis=-1, keepdims=True)
    yc = y - mu
    var = jnp.mean(yc * yc, axis=-1, keepdims=True)
    return yc * lax.rsqrt(var + LN_EPS) * g + b


def _mm_bias_kernel(x_ref, w_ref, b_ref, o_ref):
    acc = _dot(x_ref[...], w_ref[...])
    o_ref[...] = (acc + b_ref[...]).astype(o_ref.dtype)


def _mm_bias_scale_kernel(x_ref, w_ref, b_ref, s_ref, o_ref):
    acc = _dot(x_ref[...], w_ref[...])
    o_ref[...] = ((acc + b_ref[...]) * s_ref[...]).astype(o_ref.dtype)


def matmul_bias(xb, w, b, out_dtype, tm, tn, name, col_scale=None):
    m, k = xb.shape
    n = w.shape[1]
    vec = pl.BlockSpec((1, tn), lambda i, j: (0, j))
    extra = () if col_scale is None else (col_scale,)
    return pl.pallas_call(
        _mm_bias_kernel if col_scale is None else _mm_bias_scale_kernel,
        grid=(m // tm, n // tn),
        in_specs=[pl.BlockSpec((tm, k), lambda i, j: (i, 0)),
                  pl.BlockSpec((k, tn), lambda i, j: (0, j)), vec] + [vec] * len(extra),
        out_specs=pl.BlockSpec((tm, tn), lambda i, j: (i, j)),
        out_shape=jax.ShapeDtypeStruct((m, n), out_dtype),
        compiler_params=_params("parallel", "arbitrary"),
        name=name,
    )(xb, w, b, *extra)


AUX_BLOCK = 256


def _aux_kernel(z_ref, alog_ref, dt_ref, o_ref, carry_ref):
    @pl.when(pl.program_id(1) == 0)
    def _():
        carry_ref[...] = jnp.zeros_like(carry_ref)

    z = z_ref[0]
    lane = lax.broadcasted_iota(jnp.int32, z.shape, 1)
    log_f = -_softplus(-z)
    g = -jnp.exp(alog_ref[...]) * _softplus(z + dt_ref[...])
    beta = _sigmoid(z)

    ri = lax.broadcasted_iota(jnp.int32, (AUX_BLOCK, AUX_BLOCK), 0)
    ci = lax.broadcasted_iota(jnp.int32, (AUX_BLOCK, AUX_BLOCK), 1)
    tri = ri >= ci
    ones_seq = jnp.where(tri, 1.0, 0.0).astype(BF16)
    ones_chunk = jnp.where(tri & ((ri // CHUNK) == (ci // CHUNK)), 1.0, 0.0).astype(BF16)

    def cumsum(ones, val):
        hi, mid, lo = _split3(val)
        return _dot(ones, hi) + _dot(ones, mid) + _dot(ones, lo)

    c = cumsum(ones_seq, log_f) + carry_ref[...]
    carry_ref[...] = c[AUX_BLOCK - 1:AUX_BLOCK, :]
    gc = cumsum(ones_chunk, g)
    o_ref[0] = jnp.where(lane < AUX_GC, c, jnp.where(lane < AUX_BETA, gc, beta))


def aux_scalars(z, alog_pad, dt_pad):
    b, s, _ = z.shape
    return pl.pallas_call(
        _aux_kernel,
        grid=(b, s // AUX_BLOCK),
        in_specs=[pl.BlockSpec((1, AUX_BLOCK, AUX_LANES), lambda i, j: (i, j, 0)),
                  pl.BlockSpec((1, AUX_LANES), lambda i, j: (0, 0)),
                  pl.BlockSpec((1, AUX_LANES), lambda i, j: (0, 0))],
        out_specs=pl.BlockSpec((1, AUX_BLOCK, AUX_LANES), lambda i, j: (i, j, 0)),
        out_shape=jax.ShapeDtypeStruct((b, s, AUX_LANES), F32),
        scratch_shapes=[pltpu.VMEM((1, AUX_LANES), F32)],
        compiler_params=_params("parallel", "arbitrary"),
        name="aux_scalars",
    )(z, alog_pad, dt_pad)


def _pick_lane(x, idx):
    lane = lax.broadcasted_iota(jnp.int32, x.shape, 1)
    return jnp.sum(jnp.where(lane == idx, x, 0.0), axis=1, keepdims=True)


LOG2E = 1.4426950408889634
FOX_T = 512
FOX_QSCALE = LOG2E * HEAD_DIM ** -0.5
FOX_BIAS_TERMS = 3
FOX_GROUP = 4


def _fox_kernel(qt_ref, k_ref, vt_ref, aux_ref, crow_ref, o_ref, kaug_ref):
    hg = pl.program_id(1)
    qi = pl.program_id(2)
    nk = kaug_ref.shape[1] // FOX_T
    group = range(FOX_GROUP)

    @pl.when(qi == 0)
    def _():
        lane = lax.broadcasted_iota(jnp.int32, (FOX_T, HEAD_DIM), 1)
        for g in group:
            for j in range(nk):
                rows = slice(j * FOX_T, (j + 1) * FOX_T)
                ck = -LOG2E * _pick_lane(aux_ref[0, rows, :], AUX_C + hg * FOX_GROUP + g)
                pieces = [t.astype(F32) for t in _split3(ck)]
                bias = jnp.zeros((FOX_T, HEAD_DIM), F32)
                for i in reversed(range(FOX_BIAS_TERMS)):
                    bias = jnp.where(lane == i, pieces[i], bias)
                kaug_ref[g, rows, 0:HEAD_DIM] = k_ref[0, rows, g * HEAD_DIM:(g + 1) * HEAD_DIM]
                kaug_ref[g, rows, HEAD_DIM:2 * HEAD_DIM] = bias.astype(BF16)

    sub = lax.broadcasted_iota(jnp.int32, (HEAD_DIM, FOX_T), 0)
    ones = jnp.where(sub < FOX_BIAS_TERMS, 1.0, 0.0).astype(BF16)
    q_aug = [jnp.concatenate([qt_ref[0, g, 0], ones], axis=0) for g in group]
    cq = [LOG2E * crow_ref[g, pl.ds(qi, 1), :] for g in group]

    def block(j, carry, diagonal):
        start = pl.multiple_of(j * FOX_T, FOX_T)
        scores = [_dot(kaug_ref[g, pl.ds(start, FOX_T), :], q_aug[g]) for g in group]
        stats, probs = [], []
        for g in group:
            m, l, _ = carry[g]
            s = scores[g]
            if diagonal:
                key = lax.broadcasted_iota(jnp.int32, s.shape, 0)
                qry = lax.broadcasted_iota(jnp.int32, s.shape, 1)
                s = jnp.where(qry >= key, s, NEG_INF)
            m_new = jnp.maximum(m, jnp.max(s, axis=0, keepdims=True) + cq[g])
            alpha = jnp.exp2(m - m_new)
            p = jnp.exp2(s + (cq[g] - m_new))
            stats.append((m_new, alpha * l + jnp.sum(p, axis=0, keepdims=True), alpha))
            probs.append(p.astype(BF16))
        out = []
        for g in group:
            m_new, l, alpha = stats[g]
            acc = alpha * carry[g][2] + _dot(vt_ref[0, g, j], probs[g])
            out.append((m_new, l, acc))
        return tuple(out)

    init = tuple((jnp.full((1, FOX_T), NEG_INF, F32), jnp.zeros((1, FOX_T), F32),
                  jnp.zeros((HEAD_DIM, FOX_T), F32)) for _ in group)
    carry = lax.fori_loop(0, qi, lambda j, c: block(j, c, False), init)
    carry = block(qi, carry, True)
    for g in group:
        _, l, acc = carry[g]
        o_ref[0, :, g * HEAD_DIM:(g + 1) * HEAD_DIM] = (acc / l).T.astype(o_ref.dtype)


def fox_attention(qkv, aux, c_row):
    b, s, _ = qkv.shape
    nt = s // FOX_T
    groups = N_HEADS // FOX_GROUP

    def head_major_t(cols):
        t = qkv[:, :, cols * WIDTH:(cols + 1) * WIDTH].reshape(b, nt, FOX_T, N_HEADS, HEAD_DIM)
        return jnp.transpose(t, (0, 3, 1, 4, 2))

    return pl.pallas_call(
        _fox_kernel,
        grid=(b, groups, nt),
        in_specs=[pl.BlockSpec((1, FOX_GROUP, 1, HEAD_DIM, FOX_T), lambda i, h, t: (i, h, t, 0, 0)),
                  pl.BlockSpec((1, s, FOX_GROUP * HEAD_DIM), lambda i, h, t: (i, 0, groups + h)),
                  pl.BlockSpec((1, FOX_GROUP, nt, HEAD_DIM, FOX_T), lambda i, h, t: (i, h, 0, 0, 0)),
                  pl.BlockSpec((1, s, AUX_LANES), lambda i, h, t: (i, 0, 0)),
                  pl.BlockSpec((FOX_GROUP, nt, FOX_T), lambda i, h, t: (i * groups + h, 0, 0))],
        out_specs=pl.BlockSpec((1, FOX_T, FOX_GROUP * HEAD_DIM), lambda i, h, t: (i, t, h)),
        out_shape=jax.ShapeDtypeStruct((b, s, WIDTH), BF16),
        scratch_shapes=[pltpu.VMEM((FOX_GROUP, s, 2 * HEAD_DIM), BF16)],
        compiler_params=_params("parallel", "parallel", "arbitrary"),
        name="fox_attention",
    )(head_major_t(0), qkv, head_major_t(2), aux, c_row)


SGU_BLOCK = 4 * SGU_SPAN


def _sgu_kernel(u_ref, v_ref, g_ref, b_ref, w_ref, bst_ref, o_ref):
    ri = lax.broadcasted_iota(jnp.int32, (SGU_SPAN, SGU_SPAN), 0)
    ci = lax.broadcasted_iota(jnp.int32, (SGU_SPAN, SGU_SPAN), 1)
    mask = (ri // CHUNK) >= (ci // CHUNK)
    for g in range(N_HEADS):
        cols = slice(g * HEAD_DIM, (g + 1) * HEAD_DIM)
        wm = jnp.where(mask, w_ref[g], 0.0).astype(BF16)
        spans = [slice(n * SGU_SPAN, (n + 1) * SGU_SPAN) for n in range(SGU_BLOCK // SGU_SPAN)]
        vns = [_layer_norm_rows(v_ref[0, rows, cols], g_ref[:, cols], b_ref[:, cols]).astype(BF16)
               for rows in spans]
        mixed = [_dot(wm, vn) + bst_ref[:, g:g + 1] for vn in vns]
        for rows, mx in zip(spans, mixed):
            o_ref[0, rows, cols] = (u_ref[0, rows, cols] * mx).astype(o_ref.dtype)


def spatial_gating(uv, ln_g, ln_b, w_s, b_s_t):
    b, s, _ = uv.shape
    return pl.pallas_call(
        _sgu_kernel,
        grid=(b, s // SGU_BLOCK),
        in_specs=[pl.BlockSpec((1, SGU_BLOCK, WIDTH), lambda i, j: (i, j, 0)),
                  pl.BlockSpec((1, SGU_BLOCK, WIDTH), lambda i, j: (i, j, 1)),
                  pl.BlockSpec((1, WIDTH), lambda i, j: (0, 0)),
                  pl.BlockSpec((1, WIDTH), lambda i, j: (0, 0)),
                  pl.BlockSpec((N_HEADS, SGU_SPAN, SGU_SPAN), lambda i, j: (0, 0, 0)),
                  pl.BlockSpec((SGU_SPAN, N_HEADS), lambda i, j: (0, 0))],
        out_specs=pl.BlockSpec((1, SGU_BLOCK, WIDTH), lambda i, j: (i, j, 0)),
        out_shape=jax.ShapeDtypeStruct((b, s, WIDTH), BF16),
        compiler_params=_params("parallel", "parallel"),
        name="spatial_gating",
    )(uv, uv, ln_g, ln_b, w_s, b_s_t)


GDN_PAIR = 2 * CHUNK
GDN_HALO = 8
GDN_GROUP = N_HEADS


def _gdn_prep_kernel(x_ref, halo_ref, cw_ref, aux_ref, gcrow_ref, u_ref, w_ref, qd_ref, kdt_ref, qk_ref,
                     pad_ref):
    first = pl.program_id(1) == 0
    halo = halo_ref[0]
    pad_ref[0:GDN_HALO, :] = jnp.where(first, jnp.zeros_like(halo), halo)
    pad_ref[GDN_HALO:GDN_HALO + GDN_PAIR, :] = x_ref[0]

    def conv_silu(cols):
        y = cw_ref[GDN_CONV - 1:GDN_CONV, cols] * pad_ref[GDN_HALO:GDN_HALO + GDN_PAIR, cols]
        for j in range(GDN_CONV - 1):
            off = GDN_HALO - (GDN_CONV - 1) + j
            y = y + cw_ref[j:j + 1, cols] * pad_ref[off:off + GDN_PAIR, cols]
        return y * _sigmoid(y)

    def l2n(t):
        return t * lax.rsqrt(jnp.sum(t * t, axis=-1, keepdims=True) + RMS_EPS)

    ri = lax.broadcasted_iota(jnp.int32, (GDN_PAIR, GDN_PAIR), 0)
    ci = lax.broadcasted_iota(jnp.int32, (GDN_PAIR, GDN_PAIR), 1)
    same_chunk = (ri // CHUNK) == (ci // CHUNK)
    causal = same_chunk & (ri >= ci)
    strict = same_chunk & (ri > ci)
    row = lax.broadcasted_iota(jnp.int32, (GDN_PAIR, 1), 0)
    aux = aux_ref[0]

    for g0 in range(0, N_HEADS, GDN_GROUP):
        _gdn_prep_group(range(g0, g0 + GDN_GROUP), conv_silu, l2n, aux, gcrow_ref, causal, strict, row,
                        u_ref, w_ref, qd_ref, kdt_ref, qk_ref)


def _gdn_prep_group(heads, conv_silu, l2n, aux, gcrow_ref, causal, strict, row,
                    u_ref, w_ref, qd_ref, kdt_ref, qk_ref):
    head_cols = {h: slice(h * HEAD_DIM, (h + 1) * HEAD_DIM) for h in heads}
    ys, ps, rhss = {}, {}, {}
    for h in heads:
        cols = head_cols[h]
        q = l2n(conv_silu(slice(h * HEAD_DIM, (h + 1) * HEAD_DIM))) * (HEAD_DIM ** -0.5)
        k = l2n(conv_silu(slice(WIDTH + h * HEAD_DIM, WIDTH + (h + 1) * HEAD_DIM)))
        v = conv_silu(slice(2 * WIDTH + h * HEAD_DIM, 2 * WIDTH + (h + 1) * HEAD_DIM))
        gc = aux[:, AUX_GC + h:AUX_GC + h + 1]
        beta = aux[:, AUX_BETA + h:AUX_BETA + h + 1]
        gc_r = gcrow_ref[0, h:h + 1, :]
        decay = jnp.exp(jnp.where(causal, gc - gc_r, NEG_INF))
        kb = k * beta
        kbf = k.astype(BF16)
        prod = _dot_nt(jnp.concatenate([kb, q], axis=0).astype(BF16), kbf)
        a = jnp.where(strict, prod[:GDN_PAIR] * decay, 0.0)
        qk_ref[0, :, cols] = (prod[GDN_PAIR:] * decay).astype(BF16)
        egc = jnp.exp(gc)
        rhss[h] = jnp.concatenate([v * beta, kb * egc], axis=1)
        qd_ref[0, :, cols] = (q * egc).astype(BF16)
        g_last = jnp.where(row < CHUNK, gc[CHUNK - 1:CHUNK, :], gc[GDN_PAIR - 1:GDN_PAIR, :])
        kdt_ref[0, cols, :] = (k * jnp.exp(g_last - gc)).T.astype(BF16)
        ab = a.astype(BF16)
        ys[h] = -a
        ps[h] = _dot(ab, ab)

    n = 2
    while True:
        pbs = {h: ps[h].astype(BF16) for h in heads}
        ys = {h: ys[h] + ps[h] + _dot(ys[h].astype(BF16), pbs[h]) for h in heads}
        n *= 2
        if n >= CHUNK:
            break
        ps = {h: _dot(pbs[h], pbs[h]) for h in heads}

    for h in heads:
        sol = rhss[h] + _dot(ys[h].astype(BF16), rhss[h].astype(BF16))
        u_ref[0, :, head_cols[h]] = sol[:, :HEAD_DIM]
        w_ref[0, :, head_cols[h]] = sol[:, HEAD_DIM:].astype(BF16)


def gdn_prepare(qkv, conv_w, aux, gc_row):
    b, s, _ = qkv.shape
    halo_blocks = GDN_PAIR // GDN_HALO
    tok = pl.BlockSpec((1, GDN_PAIR, WIDTH), lambda i, t: (i, t, 0))
    wide = lambda dt: jax.ShapeDtypeStruct((b, s, WIDTH), dt)
    return pl.pallas_call(
        _gdn_prep_kernel,
        grid=(b, s // GDN_PAIR),
        in_specs=[pl.BlockSpec((1, GDN_PAIR, 3 * WIDTH), lambda i, t: (i, t, 0)),
                  pl.BlockSpec((1, GDN_HALO, 3 * WIDTH),
                               lambda i, t: (i, jnp.maximum(t * halo_blocks - 1, 0), 0)),
                  pl.BlockSpec((GDN_CONV, 3 * WIDTH), lambda i, t: (0, 0)),
                  pl.BlockSpec((1, GDN_PAIR, AUX_LANES), lambda i, t: (i, t, 0)),
                  pl.BlockSpec((1, N_HEADS, GDN_PAIR), lambda i, t: (i, 0, t))],
        out_specs=[tok, tok, tok, pl.BlockSpec((1, WIDTH, GDN_PAIR), lambda i, t: (i, 0, t)), tok],
        out_shape=[wide(F32), wide(BF16), wide(BF16), jax.ShapeDtypeStruct((b, WIDTH, s), BF16), wide(BF16)],
        scratch_shapes=[pltpu.VMEM((GDN_HALO + GDN_PAIR, 3 * WIDTH), F32)],
        compiler_params=_params("parallel", "parallel"),
        name="gdn_prepare",
    )(qkv, qkv, conv_w, aux, gc_row)


SCAN_BLOCK = 256


def _gdn_scan_kernel(u_ref, w_ref, qd_ref, qk_ref, kdt_ref, aux_ref, gate_ref, ng_ref, o_ref, state_ref):
    @pl.when(pl.program_id(1) == 0)
    def _():
        state_ref[...] = jnp.zeros_like(state_ref)

    zeros = jnp.zeros((CHUNK, HEAD_DIM), BF16)
    for c in range(SCAN_BLOCK // CHUNK):
        rows = slice(c * CHUNK, (c + 1) * CHUNK)
        pair = slice((c // 2) * GDN_PAIR, (c // 2 + 1) * GDN_PAIR)
        last = (c + 1) * CHUNK - 1
        heads = range(N_HEADS)
        head_cols = [slice(h * HEAD_DIM, (h + 1) * HEAD_DIM) for h in heads]
        states = [state_ref[h] for h in heads]
        r1 = [_dot(jnp.concatenate([w_ref[0, rows, head_cols[h]], qd_ref[0, rows, head_cols[h]]], axis=0),
                   states[h].astype(BF16)) for h in heads]
        r2 = []
        for h in heads:
            v_new = (u_ref[0, rows, head_cols[h]] - r1[h][:CHUNK]).astype(BF16)
            v_pad = jnp.concatenate([v_new, zeros] if c % 2 == 0 else [zeros, v_new], axis=0)
            r2.append(_dot(jnp.concatenate([qk_ref[0, rows, head_cols[h]], kdt_ref[0, head_cols[h], pair]],
                                           axis=0), v_pad))
        for h in heads:
            decay_last = jnp.exp(aux_ref[0, last:last + 1, AUX_GC + h:AUX_GC + h + 1])
            state_ref[h] = states[h] * decay_last + r2[h][CHUNK:]
        for h in heads:
            o = r1[h][CHUNK:] + r2[h][:CHUNK]
            o = o * lax.rsqrt(jnp.mean(o * o, axis=-1, keepdims=True) + RMS_EPS) * ng_ref[...]
            gt = gate_ref[0, rows, head_cols[h]]
            o_ref[0, rows, head_cols[h]] = (o * (gt * _sigmoid(gt))).astype(o_ref.dtype)


def gdn_scan(u, w, qd, qk, kdt, aux, gate, norm_g):
    b, s, _ = u.shape
    tok = pl.BlockSpec((1, SCAN_BLOCK, WIDTH), lambda i, t: (i, t, 0))
    return pl.pallas_call(
        _gdn_scan_kernel,
        grid=(b, s // SCAN_BLOCK),
        in_specs=[tok, tok, tok, tok,
                  pl.BlockSpec((1, WIDTH, SCAN_BLOCK), lambda i, t: (i, 0, t)),
                  pl.BlockSpec((1, SCAN_BLOCK, AUX_LANES), lambda i, t: (i, t, 0)),
                  tok, pl.BlockSpec((1, HEAD_DIM), lambda i, t: (0, 0))],
        out_specs=tok,
        out_shape=jax.ShapeDtypeStruct((b, s, WIDTH), BF16),
        scratch_shapes=[pltpu.VMEM((N_HEADS, HEAD_DIM, HEAD_DIM), F32)],
        compiler_params=_params("parallel", "arbitrary"),
        name="gdn_scan",
    )(u, w, qd, qk, kdt, aux, gate, norm_g)


def gdn_mixer(qkv, conv_w, aux, gc_row, gate, norm_g):
    u, w, qd, kdt, qk = gdn_prepare(qkv, conv_w, aux, gc_row)
    return gdn_scan(u, w, qd, qk, kdt, aux, gate, norm_g)


MERGE_TM = 512
MERGE_TN = 512


def _merge_kernel(xb_ref, ya_ref, yb_ref, yc_ref, wg0, wg1, wg2, bg0, bg1, bg2, wa, wb, wc, o_ref):
    xb = xb_ref[...]

    def branch(y_ref, wp, wg, bg):
        return _sigmoid(_dot(xb, wg[...]) + bg[...]) * _dot(y_ref[...], wp[...])

    merged = branch(ya_ref, wa, wg0, bg0) + branch(yb_ref, wb, wg1, bg1) + branch(yc_ref, wc, wg2, bg2)
    o_ref[...] = merged.astype(o_ref.dtype)


def merge_branches(xb, ya, yb, yc, wg, bg, wp):
    m = xb.shape[0]
    row = lambda width: pl.BlockSpec((MERGE_TM, width), lambda i, j: (i, 0))
    col = lambda depth: pl.BlockSpec((depth, MERGE_TN), lambda i, j: (0, j))
    return pl.pallas_call(
        _merge_kernel,
        grid=(m // MERGE_TM, D_MODEL // MERGE_TN),
        in_specs=[row(D_MODEL), row(WIDTH), row(WIDTH), row(WIDTH),
                  col(D_MODEL), col(D_MODEL), col(D_MODEL), col(1), col(1), col(1),
                  col(WIDTH), col(WIDTH), col(WIDTH)],
        out_specs=pl.BlockSpec((MERGE_TM, MERGE_TN), lambda i, j: (i, j)),
        out_shape=jax.ShapeDtypeStruct((m, D_MODEL), BF16),
        compiler_params=_params("parallel", "arbitrary"),
        name="merge_branches",
    )(xb, ya, yb, yc, *wg, *bg, *wp)


OUT_TM = 256


def _proj_ln_kernel(y_ref, w_ref, x_ref, g_ref, b_ref, o_ref, ob_ref):
    r = DEEPNORM_ALPHA * x_ref[...] + _dot(y_ref[...], w_ref[...])
    out = _layer_norm_rows(r, g_ref[...], b_ref[...])
    o_ref[...] = out
    ob_ref[...] = out.astype(BF16)


def proj_residual_ln(y, w, x, g, b):
    m, k = y.shape
    vec = pl.BlockSpec((1, D_MODEL), lambda i: (0, 0))
    row = pl.BlockSpec((OUT_TM, D_MODEL), lambda i: (i, 0))
    return pl.pallas_call(
        _proj_ln_kernel,
        grid=(m // OUT_TM,),
        in_specs=[pl.BlockSpec((OUT_TM, k), lambda i: (i, 0)),
                  pl.BlockSpec((k, D_MODEL), lambda i: (0, 0)),
                  row, vec, vec],
        out_specs=[row, row],
        out_shape=[jax.ShapeDtypeStruct((m, D_MODEL), F32), jax.ShapeDtypeStruct((m, D_MODEL), BF16)],
        compiler_params=_params("parallel"),
        name="proj_residual_ln",
    )(y, w, x, g, b)


DOWN_TM = 512
DOWN_TK = D_FF_PAD // 4


def _ffn_down_kernel(a_ref, w_ref, x_ref, g_ref, b_ref, o_ref, ob_ref, acc_ref):
    kk = pl.program_id(1)

    @pl.when(kk == 0)
    def _():
        acc_ref[...] = DEEPNORM_ALPHA * x_ref[...]

    acc_ref[...] += _dot(a_ref[...], w_ref[...])

    @pl.when(kk == pl.num_programs(1) - 1)
    def _():
        out = _layer_norm_rows(acc_ref[...], g_ref[...], b_ref[...])
        o_ref[...] = out
        ob_ref[...] = out.astype(BF16)


def ffn_down_ln(act, w, x, g, b):
    m, k = act.shape
    vec = pl.BlockSpec((1, D_MODEL), lambda i, j: (0, 0))
    row = pl.BlockSpec((DOWN_TM, D_MODEL), lambda i, j: (i, 0))
    return pl.pallas_call(
        _ffn_down_kernel,
        grid=(m // DOWN_TM, k // DOWN_TK),
        in_specs=[pl.BlockSpec((DOWN_TM, DOWN_TK), lambda i, j: (i, j)),
                  pl.BlockSpec((DOWN_TK, D_MODEL), lambda i, j: (j, 0)),
                  row, vec, vec],
        out_specs=[row, row],
        out_shape=[jax.ShapeDtypeStruct((m, D_MODEL), F32), jax.ShapeDtypeStruct((m, D_MODEL), BF16)],
        scratch_shapes=[pltpu.VMEM((DOWN_TM, D_MODEL), F32)],
        compiler_params=_params("parallel", "arbitrary"),
        name="ffn_down_ln",
    )(act, w, x, g, b)


FFN_TM = 1024
FFN_TN = 512
FFN_HALO = 16


def _stage_rows_with_halo(x_ref, halo_ref, xs_ref, first):
    halo = halo_ref[...]
    xs_ref[0:FFN_HALO, :] = jnp.where(first, jnp.zeros_like(halo), halo)
    xs_ref[FFN_HALO:, :] = x_ref[...]


def _causal_taps(r, cw, taps):
    tm = r.shape[0] - FFN_HALO
    y = cw[taps - 1:taps] * r[FFN_HALO:FFN_HALO + tm]
    for j in range(taps - 1):
        off = FFN_HALO - (taps - 1) + j
        y = y + cw[j:j + 1] * r[off:off + tm]
    return y


def _ffn_up_kernel(x_ref, halo_ref, wg_ref, wv_ref, cwg_ref, cwv_ref, cbg_ref, cbv_ref, o_ref, xs_ref,
                   *, tiles_per_seq):
    @pl.when(pl.program_id(1) == 0)
    def _():
        _stage_rows_with_halo(x_ref, halo_ref, xs_ref, (pl.program_id(0) % tiles_per_seq) == 0)

    xs = xs_ref[...]
    hg = cbg_ref[...] + _causal_taps(_dot(xs, wg_ref[...]), cwg_ref[...], FFN_CONV)
    hv = cbv_ref[...] + _causal_taps(_dot(xs, wv_ref[...]), cwv_ref[...], FFN_CONV)
    o_ref[...] = (hg * _sigmoid(hg) * hv).astype(o_ref.dtype)


def ffn_up(xb, seq_len, wg, wv, cwg, cwv, cbg, cbv):
    m, k = xb.shape
    n = wg.shape[1]
    assert seq_len % FFN_TM == 0
    halo_blocks = FFN_TM // FFN_HALO
    col = lambda depth: pl.BlockSpec((depth, FFN_TN), lambda i, j: (0, j))
    return pl.pallas_call(
        functools.partial(_ffn_up_kernel, tiles_per_seq=seq_len // FFN_TM),
        grid=(m // FFN_TM, n // FFN_TN),
        in_specs=[pl.BlockSpec((FFN_TM, k), lambda i, j: (i, 0)),
                  pl.BlockSpec((FFN_HALO, k), lambda i, j: (jnp.maximum(i * halo_blocks - 1, 0), 0)),
                  col(k), col(k), col(FFN_CONV), col(FFN_CONV), col(1), col(1)],
        out_specs=pl.BlockSpec((FFN_TM, FFN_TN), lambda i, j: (i, j)),
        out_shape=jax.ShapeDtypeStruct((m, n), BF16),
        scratch_shapes=[pltpu.VMEM((FFN_HALO + FFN_TM, k), BF16)],
        compiler_params=_params("parallel", "arbitrary"),
        name="ffn_up",
    )(xb, xb, wg, wv, cwg, cwv, cbg, cbv)


_FOX_QKV = (0, 3 * WIDTH)
_FOX_F = (_FOX_QKV[1], _FOX_QKV[1] + N_HEADS)
_SGU_UV = (_FOX_F[1], _FOX_F[1] + 2 * WIDTH)
_GDN_QKV = (_SGU_UV[1], _SGU_UV[1] + 3 * WIDTH)
_GDN_A = (_GDN_QKV[1], _GDN_QKV[1] + N_HEADS)
_GDN_B = (_GDN_A[1], _GDN_A[1] + N_HEADS)
_GDN_GATE = (_GDN_B[1], _GDN_B[1] + WIDTH)
_GATES = (_GDN_GATE[1], _GDN_GATE[1] + 3 * D_MODEL)


def _cols(w, span):
    return w[..., span[0]:span[1]]


def _pad_lanes(v, offset):
    return jnp.zeros((1, AUX_LANES), F32).at[0, offset:offset + v.shape[0]].set(v)


def _layer(x, xb, bsz, seq, w_in, b_in, sgu_ln_g, sgu_ln_b, sgu_w, sgu_b, gdn_conv_w, gdn_a_log,
           gdn_dt_bias, gdn_norm_g, w_proj_a, w_proj_b, w_proj_c, w_out, ln1_g, ln1_b,
           ffn_w_up, ffn_conv_w, ffn_conv_b, ffn_w_down, ln2_g, ln2_b):
    tokens = bsz * seq
    wb = lambda span: _cols(w_in, span).astype(BF16)
    bias = lambda span: _cols(b_in, span)[None, :]

    q_scale = jnp.concatenate([jnp.full((1, WIDTH), FOX_QSCALE, F32), jnp.ones((1, 2 * WIDTH), F32)], axis=1)
    fox_qkv = matmul_bias(xb, wb(_FOX_QKV), bias(_FOX_QKV), BF16, 1024, 1024, "in_fox", col_scale=q_scale)
    sgu_uv = matmul_bias(xb, wb(_SGU_UV), bias(_SGU_UV), F32, 1024, 1024, "in_sgu")
    gdn_qkv = matmul_bias(xb, wb(_GDN_QKV), bias(_GDN_QKV), F32, 1024, 1024, "in_gdn")
    gdn_gate = matmul_bias(xb, wb(_GDN_GATE), bias(_GDN_GATE), F32, 1024, 1024, "in_gdn_gate")
    pad = AUX_LANES - 3 * N_HEADS
    w_small = jnp.concatenate([_cols(w_in, _FOX_F), _cols(w_in, _GDN_A), _cols(w_in, _GDN_B),
                               jnp.zeros((D_MODEL, pad), F32)], axis=1).astype(BF16)
    b_small = jnp.concatenate([_cols(b_in, _FOX_F), _cols(b_in, _GDN_A), _cols(b_in, _GDN_B),
                               jnp.zeros((pad,), F32)])[None, :]
    z_small = matmul_bias(xb, w_small, b_small, F32, 1024, AUX_LANES, "in_small")

    aux = aux_scalars(z_small.reshape(bsz, seq, AUX_LANES), _pad_lanes(gdn_a_log, AUX_GC),
                      _pad_lanes(gdn_dt_bias, AUX_GC))
    head_rows = lambda off: jnp.swapaxes(aux[:, :, off:off + N_HEADS], 1, 2).reshape(bsz * N_HEADS, seq)
    c_row = head_rows(AUX_C).reshape(bsz * N_HEADS, seq // FOX_T, FOX_T)
    gc_row = head_rows(AUX_GC).reshape(bsz, N_HEADS, seq)

    y_a = fox_attention(fox_qkv.reshape(bsz, seq, 3 * WIDTH), aux, c_row)
    y_b = spatial_gating(sgu_uv.reshape(bsz, seq, 2 * WIDTH), sgu_ln_g[None, :], sgu_ln_b[None, :],
                         sgu_w, sgu_b.T)
    y_c = gdn_mixer(gdn_qkv.reshape(bsz, seq, 3 * WIDTH), gdn_conv_w, aux, gc_row,
                    gdn_gate.reshape(bsz, seq, WIDTH), gdn_norm_g[None, :])

    g0 = _GATES[0]
    wg = [w_in[:, g0 + i * D_MODEL:g0 + (i + 1) * D_MODEL].astype(BF16) for i in range(3)]
    bg = [b_in[None, g0 + i * D_MODEL:g0 + (i + 1) * D_MODEL] for i in range(3)]
    wp = [w_proj_a.astype(BF16), w_proj_b.astype(BF16), w_proj_c.astype(BF16)]
    merged = merge_branches(xb, y_a.reshape(tokens, WIDTH), y_b.reshape(tokens, WIDTH),
                            y_c.reshape(tokens, WIDTH), wg, bg, wp)
    x, xb = proj_residual_ln(merged, w_out.astype(BF16), x, ln1_g[None, :], ln1_b[None, :])

    ff_pad = D_FF_PAD - D_FF
    pad_cols = lambda t: jnp.pad(t, ((0, 0), (0, ff_pad)))
    act = ffn_up(xb, seq,
                 pad_cols(ffn_w_up[:, :D_FF].astype(BF16)), pad_cols(ffn_w_up[:, D_FF:].astype(BF16)),
                 pad_cols(ffn_conv_w[:, :D_FF]), pad_cols(ffn_conv_w[:, D_FF:]),
                 pad_cols(ffn_conv_b[None, :D_FF]), pad_cols(ffn_conv_b[None, D_FF:]))
    w_down = jnp.pad(ffn_w_down.astype(BF16), ((0, ff_pad), (0, 0)))
    return ffn_down_ln(act, w_down, x, ln2_g[None, :], ln2_b[None, :])


@jax.jit
def kernel(x, w_in, b_in, sgu_ln_g, sgu_ln_b, sgu_w, sgu_b, gdn_conv_w, gdn_a_log, gdn_dt_bias, gdn_norm_g, w_proj_a, w_proj_b, w_proj_c, w_out, ln1_g, ln1_b, ffn_w_up, ffn_conv_w, ffn_conv_b, ffn_w_down, ln2_g, ln2_b):
    bsz, seq, _ = x.shape
    stacked = (w_in, b_in, sgu_ln_g, sgu_ln_b, sgu_w, sgu_b, gdn_conv_w, gdn_a_log, gdn_dt_bias,
               gdn_norm_g, w_proj_a, w_proj_b, w_proj_c, w_out, ln1_g, ln1_b, ffn_w_up, ffn_conv_w,
               ffn_conv_b, ffn_w_down, ln2_g, ln2_b)
    x = x.reshape(bsz * seq, D_MODEL)
    xb = x.astype(BF16)
    for layer in range(w_in.shape[0]):
        x, xb = _layer(x, xb, bsz, seq, *(t[layer] for t in stacked))
    return x.reshape(bsz, seq, D_MODEL)
```

```python
import functools

import jax
import jax.numpy as jnp
from jax import lax
from jax.experimental import pallas as pl
from jax.experimental.pallas import tpu as pltpu

F32 = jnp.float32
BF16 = jnp.bfloat16

D_MODEL = 2048
HEAD_DIM = 128
N_HEADS = 8
WIDTH = N_HEADS * HEAD_DIM
CHUNK = 64
SGU_SPAN = 128
GDN_CONV = 4
D_FF = 5504
D_FF_PAD = 5632
FFN_CONV = 3
DEPTH = 4
DEEPNORM_ALPHA = (2 * DEPTH) ** 0.25
LN_EPS = 1e-5
RMS_EPS = 1e-6

AUX_LANES = 128
AUX_C = 0
AUX_GC = 8
AUX_BETA = 16

VMEM_LIMIT = 56 * 1024 * 1024

NEG_INF = float("-inf")


def _params(*semantics):
    return pltpu.CompilerParams(dimension_semantics=semantics, vmem_limit_bytes=VMEM_LIMIT)


def _dot(a, b):
    return jnp.dot(a, b, preferred_element_type=F32)


def _dot_nt(a, b):
    return lax.dot_general(a, b, (((1,), (1,)), ((), ())), preferred_element_type=F32)


def _dot_tn(a, b):
    return lax.dot_general(a, b, (((0,), (0,)), ((), ())), preferred_element_type=F32)


def _split2(x):
    hi = x.astype(BF16)
    lo = (x - hi.astype(F32)).astype(BF16)
    return hi, lo


def _split3(x):
    hi = x.astype(BF16)
    r1 = x - hi.astype(F32)
    mid = r1.astype(BF16)
    lo = (r1 - mid.astype(F32)).astype(BF16)
    return hi, mid, lo


def _mm3(a, b):
    ah, al = _split2(a)
    bh, bl = _split2(b)
    return _dot(ah, bh) + _dot(ah, bl) + _dot(al, bh)


def _softplus(z):
    return jnp.maximum(z, 0.0) + jnp.log1p(jnp.exp(-jnp.abs(z)))


def _sigmoid(z):
    return 1.0 / (1.0 + jnp.exp(-z))


def _layer_norm_rows(y, g, b):
    mu = jnp.mean(y, axis=-1, keepdims=True)
    yc = y - mu
    var = jnp.mean(yc * yc, axis=-1, keepdims=True)
    return yc * lax.rsqrt(var + LN_EPS) * g + b


def _mm_bias_kernel(x_ref, w_ref, b_ref, o_ref):
    acc = _dot(x_ref[...], w_ref[...])
    o_ref[...] = (acc + b_ref[...]).astype(o_ref.dtype)


def _mm_bias_scale_kernel(x_ref, w_ref, b_ref, s_ref, o_ref):
    acc = _dot(x_ref[...], w_ref[...])
    o_ref[...] = ((acc + b_ref[...]) * s_ref[...]).astype(o_ref.dtype)


def matmul_bias(xb, w, b, out_dtype, tm, tn, name, col_scale=None):
    m, k = xb.shape
    n = w.shape[1]
    vec = pl.BlockSpec((1, tn), lambda i, j: (0, j))
    extra = () if col_scale is None else (col_scale,)
    return pl.pallas_call(
        _mm_bias_kernel if col_scale is None else _mm_bias_scale_kernel,
        grid=(m // tm, n // tn),
        in_specs=[pl.BlockSpec((tm, k), lambda i, j: (i, 0)),
                  pl.BlockSpec((k, tn), lambda i, j: (0, j)), vec] + [vec] * len(extra),
        out_specs=pl.BlockSpec((tm, tn), lambda i, j: (i, j)),
        out_shape=jax.ShapeDtypeStruct((m, n), out_dtype),
        compiler_params=_params("parallel", "arbitrary"),
        name=name,
    )(xb, w, b, *extra)


AUX_BLOCK = 256


def _aux_kernel(z_ref, alog_ref, dt_ref, o_ref, carry_ref):
    @pl.when(pl.program_id(1) == 0)
    def _():
        carry_ref[...] = jnp.zeros_like(carry_ref)

    z = z_ref[0]
    lane = lax.broadcasted_iota(jnp.int32, z.shape, 1)
    log_f = -_softplus(-z)
    g = -jnp.exp(alog_ref[...]) * _softplus(z + dt_ref[...])
    beta = _sigmoid(z)

    ri = lax.broadcasted_iota(jnp.int32, (AUX_BLOCK, AUX_BLOCK), 0)
    ci = lax.broadcasted_iota(jnp.int32, (AUX_BLOCK, AUX_BLOCK), 1)
    tri = ri >= ci
    ones_seq = jnp.where(tri, 1.0, 0.0).astype(BF16)
    ones_chunk = jnp.where(tri & ((ri // CHUNK) == (ci // CHUNK)), 1.0, 0.0).astype(BF16)

    def cumsum(ones, val):
        hi, mid, lo = _split3(val)
        return _dot(ones, hi) + _dot(ones, mid) + _dot(ones, lo)

    c = cumsum(ones_seq, log_f) + carry_ref[...]
    carry_ref[...] = c[AUX_BLOCK - 1:AUX_BLOCK, :]
    gc = cumsum(ones_chunk, g)
    o_ref[0] = jnp.where(lane < AUX_GC, c, jnp.where(lane < AUX_BETA, gc, beta))


def aux_scalars(z, alog_pad, dt_pad):
    b, s, _ = z.shape
    return pl.pallas_call(
        _aux_kernel,
        grid=(b, s // AUX_BLOCK),
        in_specs=[pl.BlockSpec((1, AUX_BLOCK, AUX_LANES), lambda i, j: (i, j, 0)),
                  pl.BlockSpec((1, AUX_LANES), lambda i, j: (0, 0)),
                  pl.BlockSpec((1, AUX_LANES), lambda i, j: (0, 0))],
        out_specs=pl.BlockSpec((1, AUX_BLOCK, AUX_LANES), lambda i, j: (i, j, 0)),
        out_shape=jax.ShapeDtypeStruct((b, s, AUX_LANES), F32),
        scratch_shapes=[pltpu.VMEM((1, AUX_LANES), F32)],
        compiler_params=_params("parallel", "arbitrary"),
        name="aux_scalars",
    )(z, alog_pad, dt_pad)


def _pick_lane(x, idx):
    lane = lax.broadcasted_iota(jnp.int32, x.shape, 1)
    return jnp.sum(jnp.where(lane == idx, x, 0.0), axis=1, keepdims=True)


LOG2E = 1.4426950408889634
FOX_T = 512
FOX_QSCALE = LOG2E * HEAD_DIM ** -0.5
FOX_BIAS_TERMS = 3
FOX_GROUP = 4


def _fox_kernel(qt_ref, k_ref, vt_ref, aux_ref, crow_ref, o_ref, kaug_ref):
    hg = pl.program_id(1)
    qi = pl.program_id(2)
    nk = kaug_ref.shape[1] // FOX_T
    group = range(FOX_GROUP)

    @pl.when(qi == 0)
    def _():
        lane = lax.broadcasted_iota(jnp.int32, (FOX_T, HEAD_DIM), 1)
        for g in group:
            for j in range(nk):
                rows = slice(j * FOX_T, (j + 1) * FOX_T)
                ck = -LOG2E * _pick_lane(aux_ref[0, rows, :], AUX_C + hg * FOX_GROUP + g)
                pieces = [t.astype(F32) for t in _split3(ck)]
                bias = jnp.zeros((FOX_T, HEAD_DIM), F32)
                for i in reversed(range(FOX_BIAS_TERMS)):
                    bias = jnp.where(lane == i, pieces[i], bias)
                kaug_ref[g, rows, 0:HEAD_DIM] = k_ref[0, rows, g * HEAD_DIM:(g + 1) * HEAD_DIM]
                kaug_ref[g, rows, HEAD_DIM:2 * HEAD_DIM] = bias.astype(BF16)

    sub = lax.broadcasted_iota(jnp.int32, (HEAD_DIM, FOX_T), 0)
    ones = jnp.where(sub < FOX_BIAS_TERMS, 1.0, 0.0).astype(BF16)
    q_aug = [jnp.concatenate([qt_ref[0, g, 0], ones], axis=0) for g in group]
    cq = [LOG2E * crow_ref[g, pl.ds(qi, 1), :] for g in group]

    def block(j, carry, diagonal):
        start = pl.multiple_of(j * FOX_T, FOX_T)
        scores = [_dot(kaug_ref[g, pl.ds(start, FOX_T), :], q_aug[g]) for g in group]
        stats, probs = [], []
        for g in group:
            m, l, _ = carry[g]
            s = scores[g]
            if diagonal:
                key = lax.broadcasted_iota(jnp.int32, s.shape, 0)
                qry = lax.broadcasted_iota(jnp.int32, s.shape, 1)
                s = jnp.where(qry >= key, s, NEG_INF)
            m_new = jnp.maximum(m, jnp.max(s, axis=0, keepdims=True) + cq[g])
            alpha = jnp.exp2(m - m_new)
            p = jnp.exp2(s + (cq[g] - m_new))
            stats.append((m_new, alpha * l + jnp.sum(p, axis=0, keepdims=True), alpha))
            probs.append(p.astype(BF16))
        out = []
        for g in group:
            m_new, l, alpha = stats[g]
            acc = alpha * carry[g][2] + _dot(vt_ref[0, g, j], probs[g])
            out.append((m_new, l, acc))
        return tuple(out)

    init = tuple((jnp.full((1, FOX_T), NEG_INF, F32), jnp.zeros((1, FOX_T), F32),
                  jnp.zeros((HEAD_DIM, FOX_T), F32)) for _ in group)
    carry = lax.fori_loop(0, qi, lambda j, c: block(j, c, False), init)
    carry = block(qi, carry, True)
    for g in group:
        _, l, acc = carry[g]
        o_ref[0, :, g * HEAD_DIM:(g + 1) * HEAD_DIM] = (acc / l).T.astype(o_ref.dtype)


def fox_attention(qkv, aux, c_row):
    b, s, _ = qkv.shape
    nt = s // FOX_T
    groups = N_HEADS // FOX_GROUP

    def head_major_t(cols):
        t = qkv[:, :, cols * WIDTH:(cols + 1) * WIDTH].reshape(b, nt, FOX_T, N_HEADS, HEAD_DIM)
        return jnp.transpose(t, (0, 3, 1, 4, 2))

    return pl.pallas_call(
        _fox_kernel,
        grid=(b, groups, nt),
        in_specs=[pl.BlockSpec((1, FOX_GROUP, 1, HEAD_DIM, FOX_T), lambda i, h, t: (i, h, t, 0, 0)),
                  pl.BlockSpec((1, s, FOX_GROUP * HEAD_DIM), lambda i, h, t: (i, 0, groups + h)),
                  pl.BlockSpec((1, FOX_GROUP, nt, HEAD_DIM, FOX_T), lambda i, h, t: (i, h, 0, 0, 0)),
                  pl.BlockSpec((1, s, AUX_LANES), lambda i, h, t: (i, 0, 0)),
                  pl.BlockSpec((FOX_GROUP, nt, FOX_T), lambda i, h, t: (i * groups + h, 0, 0))],
        out_specs=pl.BlockSpec((1, FOX_T, FOX_GROUP * HEAD_DIM), lambda i, h, t: (i, t, h)),
        out_shape=jax.ShapeDtypeStruct((b, s, WIDTH), BF16),
        scratch_shapes=[pltpu.VMEM((FOX_GROUP, s, 2 * HEAD_DIM), BF16)],
        compiler_params=_params("parallel", "parallel", "arbitrary"),
        name="fox_attention",
    )(head_major_t(0), qkv, head_major_t(2), aux, c_row)


SGU_BLOCK = 4 * SGU_SPAN


def _sgu_kernel(u_ref, v_ref, g_ref, b_ref, w_ref, bst_ref, o_ref):
    ri = lax.broadcasted_iota(jnp.int32, (SGU_SPAN, SGU_SPAN), 0)
    ci = lax.broadcasted_iota(jnp.int32, (SGU_SPAN, SGU_SPAN), 1)
    mask = (ri // CHUNK) >= (ci // CHUNK)
    for g in range(N_HEADS):
        cols = slice(g * HEAD_DIM, (g + 1) * HEAD_DIM)
        wm = jnp.where(mask, w_ref[g], 0.0).astype(BF16)
        spans = [slice(n * SGU_SPAN, (n + 1) * SGU_SPAN) for n in range(SGU_BLOCK // SGU_SPAN)]
        vns = [_layer_norm_rows(v_ref[0, rows, cols], g_ref[:, cols], b_ref[:, cols]).astype(BF16)
               for rows in spans]
        mixed = [_dot(wm, vn) + bst_ref[:, g:g + 1] for vn in vns]
        for rows, mx in zip(spans, mixed):
            o_ref[0, rows, cols] = (u_ref[0, rows, cols] * mx).astype(o_ref.dtype)


def spatial_gating(uv, ln_g, ln_b, w_s, b_s_t):
    b, s, _ = uv.shape
    return pl.pallas_call(
        _sgu_kernel,
        grid=(b, s // SGU_BLOCK),
        in_specs=[pl.BlockSpec((1, SGU_BLOCK, WIDTH), lambda i, j: (i, j, 0)),
                  pl.BlockSpec((1, SGU_BLOCK, WIDTH), lambda i, j: (i, j, 1)),
                  pl.BlockSpec((1, WIDTH), lambda i, j: (0, 0)),
                  pl.BlockSpec((1, WIDTH), lambda i, j: (0, 0)),
                  pl.BlockSpec((N_HEADS, SGU_SPAN, SGU_SPAN), lambda i, j: (0, 0, 0)),
                  pl.BlockSpec((SGU_SPAN, N_HEADS), lambda i, j: (0, 0))],
        out_specs=pl.BlockSpec((1, SGU_BLOCK, WIDTH), lambda i, j: (i, j, 0)),
        out_shape=jax.ShapeDtypeStruct((b, s, WIDTH), BF16),
        compiler_params=_params("parallel", "parallel"),
        name="spatial_gating",
    )(uv, uv, ln_g, ln_b, w_s, b_s_t)


GDN_PAIR = 2 * CHUNK
GDN_HALO = 8
GDN_PREP_PAIRS = 2


def _gdn_prep_kernel(x_ref, halo_ref, cw_ref, aux_ref, gcrow_ref, u_ref, w_ref, qd_ref, kdt_ref, qk_ref,
                     pad_ref):
    first = pl.program_id(1) == 0
    halo = halo_ref[0]
    pad_ref[0:GDN_HALO, :] = jnp.where(first, jnp.zeros_like(halo), halo)
    pad_ref[GDN_HALO:, :] = x_ref[0]

    ri = lax.broadcasted_iota(jnp.int32, (GDN_PAIR, GDN_PAIR), 0)
    ci = lax.broadcasted_iota(jnp.int32, (GDN_PAIR, GDN_PAIR), 1)
    same_chunk = (ri // CHUNK) == (ci // CHUNK)
    causal = same_chunk & (ri >= ci)
    strict = same_chunk & (ri > ci)
    row = lax.broadcasted_iota(jnp.int32, (GDN_PAIR, 1), 0)

    for pair in range(GDN_PREP_PAIRS):
        _gdn_prep_pair(pair, pad_ref, cw_ref, aux_ref, gcrow_ref, causal, strict, row,
                       u_ref, w_ref, qd_ref, kdt_ref, qk_ref)


def _gdn_prep_pair(pair, pad_ref, cw_ref, aux_ref, gcrow_ref, causal, strict, row,
                   u_ref, w_ref, qd_ref, kdt_ref, qk_ref):
    rows = slice(pair * GDN_PAIR, (pair + 1) * GDN_PAIR)
    base = GDN_HALO + pair * GDN_PAIR

    def conv_silu(cols):
        y = cw_ref[GDN_CONV - 1:GDN_CONV, cols] * pad_ref[base:base + GDN_PAIR, cols]
        for j in range(GDN_CONV - 1):
            off = base - (GDN_CONV - 1) + j
            y = y + cw_ref[j:j + 1, cols] * pad_ref[off:off + GDN_PAIR, cols]
        return y * _sigmoid(y)

    def l2n(t):
        return t * lax.rsqrt(jnp.sum(t * t, axis=-1, keepdims=True) + RMS_EPS)

    aux = aux_ref[0, rows, :]
    heads = range(N_HEADS)
    head_cols = {h: slice(h * HEAD_DIM, (h + 1) * HEAD_DIM) for h in heads}
    ys, ps, rhss = {}, {}, {}
    for h in heads:
        cols = head_cols[h]
        q = l2n(conv_silu(slice(h * HEAD_DIM, (h + 1) * HEAD_DIM))) * (HEAD_DIM ** -0.5)
        k = l2n(conv_silu(slice(WIDTH + h * HEAD_DIM, WIDTH + (h + 1) * HEAD_DIM)))
        v = conv_silu(slice(2 * WIDTH + h * HEAD_DIM, 2 * WIDTH + (h + 1) * HEAD_DIM))
        gc = aux[:, AUX_GC + h:AUX_GC + h + 1]
        beta = aux[:, AUX_BETA + h:AUX_BETA + h + 1]
        gc_r = gcrow_ref[0, h:h + 1, rows]
        decay = jnp.exp(jnp.where(causal, gc - gc_r, NEG_INF))
        kb = k * beta
        kbf = k.astype(BF16)
        prod = _dot_nt(jnp.concatenate([kb, q], axis=0).astype(BF16), kbf)
        a = jnp.where(strict, prod[:GDN_PAIR] * decay, 0.0)
        qk_ref[0, rows, cols] = (prod[GDN_PAIR:] * decay).astype(BF16)
        egc = jnp.exp(gc)
        rhss[h] = jnp.concatenate([v * beta, kb * egc], axis=1)
        qd_ref[0, rows, cols] = (q * egc).astype(BF16)
        g_last = jnp.where(row < CHUNK, gc[CHUNK - 1:CHUNK, :], gc[GDN_PAIR - 1:GDN_PAIR, :])
        kdt_ref[0, cols, rows] = (k * jnp.exp(g_last - gc)).T.astype(BF16)
        ab = a.astype(BF16)
        ys[h] = -a
        ps[h] = _dot(ab, ab)

    n = 2
    while True:
        pbs = {h: ps[h].astype(BF16) for h in heads}
        ys = {h: ys[h] + ps[h] + _dot(ys[h].astype(BF16), pbs[h]) for h in heads}
        n *= 2
        if n >= CHUNK:
            break
        ps = {h: _dot(pbs[h], pbs[h]) for h in heads}

    for h in heads:
        sol = rhss[h] + _dot(ys[h].astype(BF16), rhss[h].astype(BF16))
        u_ref[0, rows, head_cols[h]] = sol[:, :HEAD_DIM]
        w_ref[0, rows, head_cols[h]] = sol[:, HEAD_DIM:].astype(BF16)


def gdn_prepare(qkv, conv_w, aux, gc_row):
    b, s, _ = qkv.shape
    blk = GDN_PREP_PAIRS * GDN_PAIR
    halo_blocks = blk // GDN_HALO
    tok = pl.BlockSpec((1, blk, WIDTH), lambda i, t: (i, t, 0))
    wide = lambda dt: jax.ShapeDtypeStruct((b, s, WIDTH), dt)
    return pl.pallas_call(
        _gdn_prep_kernel,
        grid=(b, s // blk),
        in_specs=[pl.BlockSpec((1, blk, 3 * WIDTH), lambda i, t: (i, t, 0)),
                  pl.BlockSpec((1, GDN_HALO, 3 * WIDTH),
                               lambda i, t: (i, jnp.maximum(t * halo_blocks - 1, 0), 0)),
                  pl.BlockSpec((GDN_CONV, 3 * WIDTH), lambda i, t: (0, 0)),
                  pl.BlockSpec((1, blk, AUX_LANES), lambda i, t: (i, t, 0)),
                  pl.BlockSpec((1, N_HEADS, blk), lambda i, t: (i, 0, t))],
        out_specs=[tok, tok, tok, pl.BlockSpec((1, WIDTH, blk), lambda i, t: (i, 0, t)), tok],
        out_shape=[wide(F32), wide(BF16), wide(BF16), jax.ShapeDtypeStruct((b, WIDTH, s), BF16), wide(BF16)],
        scratch_shapes=[pltpu.VMEM((GDN_HALO + blk, 3 * WIDTH), F32)],
        compiler_params=_params("parallel", "parallel"),
        name="gdn_prepare",
    )(qkv, qkv, conv_w, aux, gc_row)


SCAN_BLOCK = 256


def _gdn_scan_kernel(u_ref, w_ref, qd_ref, qk_ref, kdt_ref, aux_ref, gate_ref, ng_ref, o_ref, state_ref):
    @pl.when(pl.program_id(1) == 0)
    def _():
        state_ref[...] = jnp.zeros_like(state_ref)

    zeros = jnp.zeros((CHUNK, HEAD_DIM), BF16)
    for c in range(SCAN_BLOCK // CHUNK):
        rows = slice(c * CHUNK, (c + 1) * CHUNK)
        pair = slice((c // 2) * GDN_PAIR, (c // 2 + 1) * GDN_PAIR)
        last = (c + 1) * CHUNK - 1
        heads = range(N_HEADS)
        head_cols = [slice(h * HEAD_DIM, (h + 1) * HEAD_DIM) for h in heads]
        states = [state_ref[h] for h in heads]
        r1 = [_dot(jnp.concatenate([w_ref[0, rows, head_cols[h]], qd_ref[0, rows, head_cols[h]]], axis=0),
                   states[h].astype(BF16)) for h in heads]
        r2 = []
        for h in heads:
            v_new = (u_ref[0, rows, head_cols[h]] - r1[h][:CHUNK]).astype(BF16)
            v_pad = jnp.concatenate([v_new, zeros] if c % 2 == 0 else [zeros, v_new], axis=0)
            r2.append(_dot(jnp.concatenate([qk_ref[0, rows, head_cols[h]], kdt_ref[0, head_cols[h], pair]],
                                           axis=0), v_pad))
        for h in heads:
            decay_last = jnp.exp(aux_ref[0, last:last + 1, AUX_GC + h:AUX_GC + h + 1])
            state_ref[h] = states[h] * decay_last + r2[h][CHUNK:]
        for h in heads:
            o = r1[h][CHUNK:] + r2[h][:CHUNK]
            o = o * lax.rsqrt(jnp.mean(o * o, axis=-1, keepdims=True) + RMS_EPS) * ng_ref[...]
            gt = gate_ref[0, rows, head_cols[h]]
            o_ref[0, rows, head_cols[h]] = (o * (gt * _sigmoid(gt))).astype(o_ref.dtype)


def gdn_scan(u, w, qd, qk, kdt, aux, gate, norm_g):
    b, s, _ = u.shape
    tok = pl.BlockSpec((1, SCAN_BLOCK, WIDTH), lambda i, t: (i, t, 0))
    return pl.pallas_call(
        _gdn_scan_kernel,
        grid=(b, s // SCAN_BLOCK),
        in_specs=[tok, tok, tok, tok,
                  pl.BlockSpec((1, WIDTH, SCAN_BLOCK), lambda i, t: (i, 0, t)),
                  pl.BlockSpec((1, SCAN_BLOCK, AUX_LANES), lambda i, t: (i, t, 0)),
                  tok, pl.BlockSpec((1, HEAD_DIM), lambda i, t: (0, 0))],
        out_specs=tok,
        out_shape=jax.ShapeDtypeStruct((b, s, WIDTH), BF16),
        scratch_shapes=[pltpu.VMEM((N_HEADS, HEAD_DIM, HEAD_DIM), F32)],
        compiler_params=_params("parallel", "arbitrary"),
        name="gdn_scan",
    )(u, w, qd, qk, kdt, aux, gate, norm_g)


def gdn_mixer(qkv, conv_w, aux, gc_row, gate, norm_g):
    u, w, qd, kdt, qk = gdn_prepare(qkv, conv_w, aux, gc_row)
    return gdn_scan(u, w, qd, qk, kdt, aux, gate, norm_g)


MERGE_TM = 512
MERGE_TN = 512


def _merge_kernel(xb_ref, ya_ref, yb_ref, yc_ref, wg0, wg1, wg2, bg0, bg1, bg2, wa, wb, wc, o_ref):
    xb = xb_ref[...]

    def branch(y_ref, wp, wg, bg):
        return _sigmoid(_dot(xb, wg[...]) + bg[...]) * _dot(y_ref[...], wp[...])

    merged = branch(ya_ref, wa, wg0, bg0) + branch(yb_ref, wb, wg1, bg1) + branch(yc_ref, wc, wg2, bg2)
    o_ref[...] = merged.astype(o_ref.dtype)


def merge_branches(xb, ya, yb, yc, wg, bg, wp):
    m = xb.shape[0]
    row = lambda width: pl.BlockSpec((MERGE_TM, width), lambda i, j: (i, 0))
    col = lambda depth: pl.BlockSpec((depth, MERGE_TN), lambda i, j: (0, j))
    return pl.pallas_call(
        _merge_kernel,
        grid=(m // MERGE_TM, D_MODEL // MERGE_TN),
        in_specs=[row(D_MODEL), row(WIDTH), row(WIDTH), row(WIDTH),
                  col(D_MODEL), col(D_MODEL), col(D_MODEL), col(1), col(1), col(1),
                  col(WIDTH), col(WIDTH), col(WIDTH)],
        out_specs=pl.BlockSpec((MERGE_TM, MERGE_TN), lambda i, j: (i, j)),
        out_shape=jax.ShapeDtypeStruct((m, D_MODEL), BF16),
        compiler_params=_params("parallel", "arbitrary"),
        name="merge_branches",
    )(xb, ya, yb, yc, *wg, *bg, *wp)


OUT_TM = 256


def _proj_ln_kernel(y_ref, w_ref, x_ref, g_ref, b_ref, o_ref, ob_ref):
    r = DEEPNORM_ALPHA * x_ref[...] + _dot(y_ref[...], w_ref[...])
    out = _layer_norm_rows(r, g_ref[...], b_ref[...])
    o_ref[...] = out
    ob_ref[...] = out.astype(BF16)


def proj_residual_ln(y, w, x, g, b):
    m, k = y.shape
    vec = pl.BlockSpec((1, D_MODEL), lambda i: (0, 0))
    row = pl.BlockSpec((OUT_TM, D_MODEL), lambda i: (i, 0))
    return pl.pallas_call(
        _proj_ln_kernel,
        grid=(m // OUT_TM,),
        in_specs=[pl.BlockSpec((OUT_TM, k), lambda i: (i, 0)),
                  pl.BlockSpec((k, D_MODEL), lambda i: (0, 0)),
                  row, vec, vec],
        out_specs=[row, row],
        out_shape=[jax.ShapeDtypeStruct((m, D_MODEL), F32), jax.ShapeDtypeStruct((m, D_MODEL), BF16)],
        compiler_params=_params("parallel"),
        name="proj_residual_ln",
    )(y, w, x, g, b)


DOWN_TM = 512
DOWN_TK = D_FF_PAD // 4


def _ffn_down_kernel(a_ref, w_ref, x_ref, g_ref, b_ref, o_ref, ob_ref, acc_ref):
    kk = pl.program_id(1)
    last = pl.num_programs(1) - 1

    @pl.when(kk == 0)
    def _():
        acc_ref[...] = DEEPNORM_ALPHA * x_ref[...] + _dot(a_ref[...], w_ref[...])

    @pl.when((kk > 0) & (kk < last))
    def _():
        acc_ref[...] += _dot(a_ref[...], w_ref[...])

    @pl.when(kk == last)
    def _():
        out = _layer_norm_rows(acc_ref[...] + _dot(a_ref[...], w_ref[...]), g_ref[...], b_ref[...])
        o_ref[...] = out
        ob_ref[...] = out.astype(BF16)


def ffn_down_ln(act, w, x, g, b):
    m, k = act.shape
    assert k // DOWN_TK >= 2
    vec = pl.BlockSpec((1, D_MODEL), lambda i, j: (0, 0))
    row = pl.BlockSpec((DOWN_TM, D_MODEL), lambda i, j: (i, 0))
    return pl.pallas_call(
        _ffn_down_kernel,
        grid=(m // DOWN_TM, k // DOWN_TK),
        in_specs=[pl.BlockSpec((DOWN_TM, DOWN_TK), lambda i, j: (i, j)),
                  pl.BlockSpec((DOWN_TK, D_MODEL), lambda i, j: (j, 0)),
                  row, vec, vec],
        out_specs=[row, row],
        out_shape=[jax.ShapeDtypeStruct((m, D_MODEL), F32), jax.ShapeDtypeStruct((m, D_MODEL), BF16)],
        scratch_shapes=[pltpu.VMEM((DOWN_TM, D_MODEL), F32)],
        compiler_params=_params("parallel", "arbitrary"),
        name="ffn_down_ln",
    )(act, w, x, g, b)


FFN_TM = 1024
FFN_TN = 512
FFN_HALO = 16


def _stage_rows_with_halo(x_ref, halo_ref, xs_ref, first):
    halo = halo_ref[...]
    xs_ref[0:FFN_HALO, :] = jnp.where(first, jnp.zeros_like(halo), halo)
    xs_ref[FFN_HALO:, :] = x_ref[...]


def _causal_taps(r, cw, taps):
    tm = r.shape[0] - FFN_HALO
    y = cw[taps - 1:taps] * r[FFN_HALO:FFN_HALO + tm]
    for j in range(taps - 1):
        off = FFN_HALO - (taps - 1) + j
        y = y + cw[j:j + 1] * r[off:off + tm]
    return y


def _ffn_up_kernel(x_ref, halo_ref, wg_ref, wv_ref, cwg_ref, cwv_ref, cbg_ref, cbv_ref, o_ref, xs_ref,
                   *, tiles_per_seq):
    @pl.when(pl.program_id(1) == 0)
    def _():
        _stage_rows_with_halo(x_ref, halo_ref, xs_ref, (pl.program_id(0) % tiles_per_seq) == 0)

    xs = xs_ref[...]
    hg = cbg_ref[...] + _causal_taps(_dot(xs, wg_ref[...]), cwg_ref[...], FFN_CONV)
    hv = cbv_ref[...] + _causal_taps(_dot(xs, wv_ref[...]), cwv_ref[...], FFN_CONV)
    o_ref[...] = (hg * _sigmoid(hg) * hv).astype(o_ref.dtype)


def ffn_up(xb, seq_len, wg, wv, cwg, cwv, cbg, cbv):
    m, k = xb.shape
    n = wg.shape[1]
    assert seq_len % FFN_TM == 0
    halo_blocks = FFN_TM // FFN_HALO
    col = lambda depth: pl.BlockSpec((depth, FFN_TN), lambda i, j: (0, j))
    return pl.pallas_call(
        functools.partial(_ffn_up_kernel, tiles_per_seq=seq_len // FFN_TM),
        grid=(m // FFN_TM, n // FFN_TN),
        in_specs=[pl.BlockSpec((FFN_TM, k), lambda i, j: (i, 0)),
                  pl.BlockSpec((FFN_HALO, k), lambda i, j: (jnp.maximum(i * halo_blocks - 1, 0), 0)),
                  col(k), col(k), col(FFN_CONV), col(FFN_CONV), col(1), col(1)],
        out_specs=pl.BlockSpec((FFN_TM, FFN_TN), lambda i, j: (i, j)),
        out_shape=jax.ShapeDtypeStruct((m, n), BF16),
        scratch_shapes=[pltpu.VMEM((FFN_HALO + FFN_TM, k), BF16)],
        compiler_params=_params("parallel", "arbitrary"),
        name="ffn_up",
    )(xb, xb, wg, wv, cwg, cwv, cbg, cbv)


_FOX_QKV = (0, 3 * WIDTH)
_FOX_F = (_FOX_QKV[1], _FOX_QKV[1] + N_HEADS)
_SGU_UV = (_FOX_F[1], _FOX_F[1] + 2 * WIDTH)
_GDN_QKV = (_SGU_UV[1], _SGU_UV[1] + 3 * WIDTH)
_GDN_A = (_GDN_QKV[1], _GDN_QKV[1] + N_HEADS)
_GDN_B = (_GDN_A[1], _GDN_A[1] + N_HEADS)
_GDN_GATE = (_GDN_B[1], _GDN_B[1] + WIDTH)
_GATES = (_GDN_GATE[1], _GDN_GATE[1] + 3 * D_MODEL)


def _cols(w, span):
    return w[..., span[0]:span[1]]


def _pad_lanes(v, offset):
    return jnp.zeros((1, AUX_LANES), F32).at[0, offset:offset + v.shape[0]].set(v)


def _layer(x, xb, bsz, seq, w_in, b_in, sgu_ln_g, sgu_ln_b, sgu_w, sgu_b, gdn_conv_w, gdn_a_log,
           gdn_dt_bias, gdn_norm_g, w_proj_a, w_proj_b, w_proj_c, w_out, ln1_g, ln1_b,
           ffn_w_up, ffn_conv_w, ffn_conv_b, ffn_w_down, ln2_g, ln2_b):
    tokens = bsz * seq
    wb = lambda span: _cols(w_in, span).astype(BF16)
    bias = lambda span: _cols(b_in, span)[None, :]

    q_scale = jnp.concatenate([jnp.full((1, WIDTH), FOX_QSCALE, F32), jnp.ones((1, 2 * WIDTH), F32)], axis=1)
    fox_qkv = matmul_bias(xb, wb(_FOX_QKV), bias(_FOX_QKV), BF16, 1024, 1024, "in_fox", col_scale=q_scale)
    sgu_uv = matmul_bias(xb, wb(_SGU_UV), bias(_SGU_UV), F32, 1024, 1024, "in_sgu")
    gdn_qkv = matmul_bias(xb, wb(_GDN_QKV), bias(_GDN_QKV), F32, 1024, 1024, "in_gdn")
    gdn_gate = matmul_bias(xb, wb(_GDN_GATE), bias(_GDN_GATE), F32, 1024, 1024, "in_gdn_gate")
    pad = AUX_LANES - 3 * N_HEADS
    w_small = jnp.concatenate([_cols(w_in, _FOX_F), _cols(w_in, _GDN_A), _cols(w_in, _GDN_B),
                               jnp.zeros((D_MODEL, pad), F32)], axis=1).astype(BF16)
    b_small = jnp.concatenate([_cols(b_in, _FOX_F), _cols(b_in, _GDN_A), _cols(b_in, _GDN_B),
                               jnp.zeros((pad,), F32)])[None, :]
    z_small = matmul_bias(xb, w_small, b_small, F32, 1024, AUX_LANES, "in_small")

    aux = aux_scalars(z_small.reshape(bsz, seq, AUX_LANES), _pad_lanes(gdn_a_log, AUX_GC),
                      _pad_lanes(gdn_dt_bias, AUX_GC))
    head_rows = lambda off: jnp.swapaxes(aux[:, :, off:off + N_HEADS], 1, 2).reshape(bsz * N_HEADS, seq)
    c_row = head_rows(AUX_C).reshape(bsz * N_HEADS, seq // FOX_T, FOX_T)
    gc_row = head_rows(AUX_GC).reshape(bsz, N_HEADS, seq)

    y_a = fox_attention(fox_qkv.reshape(bsz, seq, 3 * WIDTH), aux, c_row)
    y_b = spatial_gating(sgu_uv.reshape(bsz, seq, 2 * WIDTH), sgu_ln_g[None, :], sgu_ln_b[None, :],
                         sgu_w, sgu_b.T)
    y_c = gdn_mixer(gdn_qkv.reshape(bsz, seq, 3 * WIDTH), gdn_conv_w, aux, gc_row,
                    gdn_gate.reshape(bsz, seq, WIDTH), gdn_norm_g[None, :])

    g0 = _GATES[0]
    wg = [w_in[:, g0 + i * D_MODEL:g0 + (i + 1) * D_MODEL].astype(BF16) for i in range(3)]
    bg = [b_in[None, g0 + i * D_MODEL:g0 + (i + 1) * D_MODEL] for i in range(3)]
    wp = [w_proj_a.astype(BF16), w_proj_b.astype(BF16), w_proj_c.astype(BF16)]
    merged = merge_branches(xb, y_a.reshape(tokens, WIDTH), y_b.reshape(tokens, WIDTH),
                            y_c.reshape(tokens, WIDTH), wg, bg, wp)
    x, xb = proj_residual_ln(merged, w_out.astype(BF16), x, ln1_g[None, :], ln1_b[None, :])

    ff_pad = D_FF_PAD - D_FF
    pad_cols = lambda t: jnp.pad(t, ((0, 0), (0, ff_pad)))
    act = ffn_up(xb, seq,
                 pad_cols(ffn_w_up[:, :D_FF].astype(BF16)), pad_cols(ffn_w_up[:, D_FF:].astype(BF16)),
                 pad_cols(ffn_conv_w[:, :D_FF]), pad_cols(ffn_conv_w[:, D_FF:]),
                 pad_cols(ffn_conv_b[None, :D_FF]), pad_cols(ffn_conv_b[None, D_FF:]))
    w_down = jnp.pad(ffn_w_down.astype(BF16), ((0, ff_pad), (0, 0)))
    return ffn_down_ln(act, w_down, x, ln2_g[None, :], ln2_b[None, :])


@jax.jit
def kernel(x, w_in, b_in, sgu_ln_g, sgu_ln_b, sgu_w, sgu_b, gdn_conv_w, gdn_a_log, gdn_dt_bias, gdn_norm_g, w_proj_a, w_proj_b, w_proj_c, w_out, ln1_g, ln1_b, ffn_w_up, ffn_conv_w, ffn_conv_b, ffn_w_down, ln2_g, ln2_b):
    bsz, seq, _ = x.shape
    stacked = (w_in, b_in, sgu_ln_g, sgu_ln_b, sgu_w, sgu_b, gdn_conv_w, gdn_a_log, gdn_dt_bias,
               gdn_norm_g, w_proj_a, w_proj_b, w_proj_c, w_out, ln1_g, ln1_b, ffn_w_up, ffn_conv_w,
               ffn_conv_b, ffn_w_down, ln2_g, ln2_b)
    x = x.reshape(bsz * seq, D_MODEL)
    xb = x.astype(BF16)
    for layer in range(w_in.shape[0]):
        x, xb = _layer(x, xb, bsz, seq, *(t[layer] for t in stacked))
    return x.reshape(bsz, seq, D_MODEL)
```

```python
import functools

import jax
import jax.numpy as jnp
from jax import lax
from jax.experimental import pallas as pl
from jax.experimental.pallas import tpu as pltpu

F32 = jnp.float32
BF16 = jnp.bfloat16

D_MODEL = 2048
HEAD_DIM = 128
N_HEADS = 8
WIDTH = N_HEADS * HEAD_DIM
CHUNK = 64
SGU_SPAN = 128
GDN_CONV = 4
D_FF = 5504
D_FF_PAD = 5632
FFN_CONV = 3
DEPTH = 4
DEEPNORM_ALPHA = (2 * DEPTH) ** 0.25
LN_EPS = 1e-5
RMS_EPS = 1e-6

AUX_LANES = 128
AUX_C = 0
AUX_GC = 8
AUX_BETA = 16

VMEM_LIMIT = 56 * 1024 * 1024

NEG_INF = float("-inf")


def _params(*semantics):
    return pltpu.CompilerParams(dimension_semantics=semantics, vmem_limit_bytes=VMEM_LIMIT)


def _dot(a, b):
    return jnp.dot(a, b, preferred_element_type=F32)


def _dot_nt(a, b):
    return lax.dot_general(a, b, (((1,), (1,)), ((), ())), preferred_element_type=F32)


def _dot_tn(a, b):
    return lax.dot_general(a, b, (((0,), (0,)), ((), ())), preferred_element_type=F32)


def _split2(x):
    hi = x.astype(BF16)
    lo = (x - hi.astype(F32)).astype(BF16)
    return hi, lo


def _split3(x):
    hi = x.astype(BF16)
    r1 = x - hi.astype(F32)
    mid = r1.astype(BF16)
    lo = (r1 - mid.astype(F32)).astype(BF16)
    return hi, mid, lo


def _mm3(a, b):
    ah, al = _split2(a)
    bh, bl = _split2(b)
    return _dot(ah, bh) + _dot(ah, bl) + _dot(al, bh)


def _softplus(z):
    return jnp.maximum(z, 0.0) + jnp.log1p(jnp.exp(-jnp.abs(z)))


def _sigmoid(z):
    return 1.0 / (1.0 + jnp.exp(-z))


def _layer_norm_rows(y, g, b):
    mu = jnp.mean(y, axis=-1, keepdims=True)
    yc = y - mu
    var = jnp.mean(yc * yc, axis=-1, keepdims=True)
    return yc * lax.rsqrt(var + LN_EPS) * g + b


def _mm_bias_kernel(x_ref, w_ref, b_ref, o_ref):
    acc = _dot(x_ref[...], w_ref[...])
    o_ref[...] = (acc + b_ref[...]).astype(o_ref.dtype)


def matmul_bias(xb, w, b, out_dtype, tm, tn, name):
    m, k = xb.shape
    n = w.shape[1]
    return pl.pallas_call(
        _mm_bias_kernel,
        grid=(m // tm, n // tn),
        in_specs=[pl.BlockSpec((tm, k), lambda i, j: (i, 0)),
                  pl.BlockSpec((k, tn), lambda i, j: (0, j)),
                  pl.BlockSpec((1, tn), lambda i, j: (0, j))],
        out_specs=pl.BlockSpec((tm, tn), lambda i, j: (i, j)),
        out_shape=jax.ShapeDtypeStruct((m, n), out_dtype),
        compiler_params=_params("parallel", "arbitrary"),
        name=name,
    )(xb, w, b)


AUX_BLOCK = 256


def _aux_kernel(z_ref, alog_ref, dt_ref, o_ref, ot_ref, carry_ref):
    @pl.when(pl.program_id(1) == 0)
    def _():
        carry_ref[...] = jnp.zeros_like(carry_ref)

    z = z_ref[0]
    lane = lax.broadcasted_iota(jnp.int32, z.shape, 1)
    log_f = -_softplus(-z)
    g = -jnp.exp(alog_ref[...]) * _softplus(z + dt_ref[...])
    beta = _sigmoid(z)

    ri = lax.broadcasted_iota(jnp.int32, (AUX_BLOCK, AUX_BLOCK), 0)
    ci = lax.broadcasted_iota(jnp.int32, (AUX_BLOCK, AUX_BLOCK), 1)
    tri = ri >= ci
    ones_seq = jnp.where(tri, 1.0, 0.0).astype(BF16)
    ones_chunk = jnp.where(tri & ((ri // CHUNK) == (ci // CHUNK)), 1.0, 0.0).astype(BF16)

    def cumsum(ones, val):
        hi, mid, lo = _split3(val)
        return _dot(ones, hi) + _dot(ones, mid) + _dot(ones, lo)

    c = cumsum(ones_seq, log_f) + carry_ref[...]
    carry_ref[...] = c[AUX_BLOCK - 1:AUX_BLOCK, :]
    gc = cumsum(ones_chunk, g)
    out = jnp.where(lane < AUX_GC, c, jnp.where(lane < AUX_BETA, gc, beta))
    o_ref[0] = out
    ot_ref[0] = out.T


def aux_scalars(z, alog_pad, dt_pad):
    b, s, n = z.shape
    last = n // AUX_LANES - 1
    return pl.pallas_call(
        _aux_kernel,
        grid=(b, s // AUX_BLOCK),
        in_specs=[pl.BlockSpec((1, AUX_BLOCK, AUX_LANES), lambda i, j: (i, j, last)),
                  pl.BlockSpec((1, AUX_LANES), lambda i, j: (0, 0)),
                  pl.BlockSpec((1, AUX_LANES), lambda i, j: (0, 0))],
        out_specs=[pl.BlockSpec((1, AUX_BLOCK, AUX_LANES), lambda i, j: (i, j, 0)),
                   pl.BlockSpec((1, AUX_LANES, AUX_BLOCK), lambda i, j: (i, 0, j))],
        out_shape=[jax.ShapeDtypeStruct((b, s, AUX_LANES), F32),
                   jax.ShapeDtypeStruct((b, AUX_LANES, s), F32)],
        scratch_shapes=[pltpu.VMEM((1, AUX_LANES), F32)],
        compiler_params=_params("parallel", "arbitrary"),
        name="aux_scalars",
    )(z, alog_pad, dt_pad)


def _pick_lane(x, idx):
    lane = lax.broadcasted_iota(jnp.int32, x.shape, 1)
    return jnp.sum(jnp.where(lane == idx, x, 0.0), axis=1, keepdims=True)


LOG2E = 1.4426950408889634
FOX_T = 512
FOX_QSCALE = LOG2E * HEAD_DIM ** -0.5
FOX_BIAS_TERMS = 3
FOX_GROUP = 4


def _fox_kernel(qt_ref, k_ref, vt_ref, aux_ref, crow_ref, o_ref, kaug_ref):
    hg = pl.program_id(1)
    qi = pl.program_id(2)
    nk = kaug_ref.shape[1] // FOX_T
    group = range(FOX_GROUP)

    @pl.when(qi == 0)
    def _():
        lane = lax.broadcasted_iota(jnp.int32, (FOX_T, HEAD_DIM), 1)
        for g in group:
            for j in range(nk):
                rows = slice(j * FOX_T, (j + 1) * FOX_T)
                ck = -LOG2E * _pick_lane(aux_ref[0, rows, :], AUX_C + hg * FOX_GROUP + g)
                pieces = [t.astype(F32) for t in _split3(ck)]
                bias = jnp.zeros((FOX_T, HEAD_DIM), F32)
                for i in reversed(range(FOX_BIAS_TERMS)):
                    bias = jnp.where(lane == i, pieces[i], bias)
                kaug_ref[g, rows, 0:HEAD_DIM] = k_ref[0, rows, g * HEAD_DIM:(g + 1) * HEAD_DIM]
                kaug_ref[g, rows, HEAD_DIM:2 * HEAD_DIM] = bias.astype(BF16)

    sub = lax.broadcasted_iota(jnp.int32, (HEAD_DIM, FOX_T), 0)
    ones = jnp.where(sub < FOX_BIAS_TERMS, 1.0, 0.0).astype(BF16)
    q_aug = [jnp.concatenate([qt_ref[0, g, 0], ones], axis=0) for g in group]
    cq = [LOG2E * crow_ref[g, pl.ds(qi, 1), :] for g in group]

    def block(j, carry, diagonal):
        start = pl.multiple_of(j * FOX_T, FOX_T)
        scores = [_dot(kaug_ref[g, pl.ds(start, FOX_T), :], q_aug[g]) for g in group]
        stats, probs = [], []
        for g in group:
            m, l, _ = carry[g]
            s = scores[g]
            if diagonal:
                key = lax.broadcasted_iota(jnp.int32, s.shape, 0)
                qry = lax.broadcasted_iota(jnp.int32, s.shape, 1)
                s = jnp.where(qry >= key, s, NEG_INF)
            m_new = jnp.maximum(m, jnp.max(s, axis=0, keepdims=True) + cq[g])
            alpha = jnp.exp2(m - m_new)
            p = jnp.exp2(s + (cq[g] - m_new))
            stats.append((m_new, alpha * l + jnp.sum(p, axis=0, keepdims=True), alpha))
            probs.append(p.astype(BF16))
        out = []
        for g in group:
            m_new, l, alpha = stats[g]
            acc = alpha * carry[g][2] + _dot(vt_ref[0, g, j], probs[g])
            out.append((m_new, l, acc))
        return tuple(out)

    init = tuple((jnp.full((1, FOX_T), NEG_INF, F32), jnp.zeros((1, FOX_T), F32),
                  jnp.zeros((HEAD_DIM, FOX_T), F32)) for _ in group)
    carry = lax.fori_loop(0, qi, lambda j, c: block(j, c, False), init)
    carry = block(qi, carry, True)
    for g in group:
        _, l, acc = carry[g]
        o_ref[0, :, g * HEAD_DIM:(g + 1) * HEAD_DIM] = (acc / l).T.astype(o_ref.dtype)


FOX_IN_TM = 1024


def _fox_in_kernel(x_ref, w_ref, b_ref, qt_ref, k_ref, vt_ref):
    j = pl.program_id(1)

    def projection():
        return _dot(x_ref[...], w_ref[...]) + b_ref[...]

    def store_transposed(dst_ref, y):
        for t in range(FOX_IN_TM // FOX_T):
            for h in range(N_HEADS):
                tile = y[t * FOX_T:(t + 1) * FOX_T, h * HEAD_DIM:(h + 1) * HEAD_DIM]
                dst_ref[0, h, t] = tile.T.astype(BF16)

    @pl.when(j == 0)
    def _():
        store_transposed(qt_ref, projection() * FOX_QSCALE)

    @pl.when(j == 1)
    def _():
        k_ref[...] = projection().astype(BF16)

    @pl.when(j == 2)
    def _():
        store_transposed(vt_ref, projection())


def fox_in_proj(xb, bsz, seq, w, b):
    m, k = xb.shape
    assert seq % FOX_IN_TM == 0
    per_seq = seq // FOX_IN_TM
    tiles = FOX_IN_TM // FOX_T
    t_spec = pl.BlockSpec((1, N_HEADS, tiles, HEAD_DIM, FOX_T),
                          lambda i, j: (i // per_seq, 0, i % per_seq, 0, 0))
    t_shape = jax.ShapeDtypeStruct((bsz, N_HEADS, seq // FOX_T, HEAD_DIM, FOX_T), BF16)
    return pl.pallas_call(
        _fox_in_kernel,
        grid=(m // FOX_IN_TM, 3),
        in_specs=[pl.BlockSpec((FOX_IN_TM, k), lambda i, j: (i, 0)),
                  pl.BlockSpec((k, WIDTH), lambda i, j: (0, j)),
                  pl.BlockSpec((1, WIDTH), lambda i, j: (0, j))],
        out_specs=[t_spec, pl.BlockSpec((FOX_IN_TM, WIDTH), lambda i, j: (i, 0)), t_spec],
        out_shape=[t_shape, jax.ShapeDtypeStruct((m, WIDTH), BF16), t_shape],
        compiler_params=_params("parallel", "arbitrary"),
        name="in_fox",
    )(xb, w, b)


def fox_attention(q_t, k, v_t, aux, c_row):
    b, s, _ = k.shape
    nt = s // FOX_T
    groups = N_HEADS // FOX_GROUP

    return pl.pallas_call(
        _fox_kernel,
        grid=(b, groups, nt),
        in_specs=[pl.BlockSpec((1, FOX_GROUP, 1, HEAD_DIM, FOX_T), lambda i, h, t: (i, h, t, 0, 0)),
                  pl.BlockSpec((1, s, FOX_GROUP * HEAD_DIM), lambda i, h, t: (i, 0, h)),
                  pl.BlockSpec((1, FOX_GROUP, nt, HEAD_DIM, FOX_T), lambda i, h, t: (i, h, 0, 0, 0)),
                  pl.BlockSpec((1, s, AUX_LANES), lambda i, h, t: (i, 0, 0)),
                  pl.BlockSpec((FOX_GROUP, nt, FOX_T), lambda i, h, t: (i * groups + h, 0, 0))],
        out_specs=pl.BlockSpec((1, FOX_T, FOX_GROUP * HEAD_DIM), lambda i, h, t: (i, t, h)),
        out_shape=jax.ShapeDtypeStruct((b, s, WIDTH), BF16),
        scratch_shapes=[pltpu.VMEM((FOX_GROUP, s, 2 * HEAD_DIM), BF16)],
        compiler_params=_params("parallel", "parallel", "arbitrary"),
        name="fox_attention",
    )(q_t, k, v_t, aux, c_row)


SGU_BLOCK = 4 * SGU_SPAN


def _sgu_kernel(u_ref, v_ref, g_ref, b_ref, w_ref, bst_ref, o_ref):
    ri = lax.broadcasted_iota(jnp.int32, (SGU_SPAN, SGU_SPAN), 0)
    ci = lax.broadcasted_iota(jnp.int32, (SGU_SPAN, SGU_SPAN), 1)
    mask = (ri // CHUNK) >= (ci // CHUNK)
    for g in range(N_HEADS):
        cols = slice(g * HEAD_DIM, (g + 1) * HEAD_DIM)
        wm = jnp.where(mask, w_ref[g], 0.0).astype(BF16)
        spans = [slice(n * SGU_SPAN, (n + 1) * SGU_SPAN) for n in range(SGU_BLOCK // SGU_SPAN)]
        vns = [_layer_norm_rows(v_ref[0, rows, cols], g_ref[:, cols], b_ref[:, cols]).astype(BF16)
               for rows in spans]
        mixed = [_dot(wm, vn) + bst_ref[:, g:g + 1] for vn in vns]
        for rows, mx in zip(spans, mixed):
            o_ref[0, rows, cols] = (u_ref[0, rows, cols] * mx).astype(o_ref.dtype)


def spatial_gating(uv, ln_g, ln_b, w_s, b_s_t):
    b, s, _ = uv.shape
    return pl.pallas_call(
        _sgu_kernel,
        grid=(b, s // SGU_BLOCK),
        in_specs=[pl.BlockSpec((1, SGU_BLOCK, WIDTH), lambda i, j: (i, j, 0)),
                  pl.BlockSpec((1, SGU_BLOCK, WIDTH), lambda i, j: (i, j, 1)),
                  pl.BlockSpec((1, WIDTH), lambda i, j: (0, 0)),
                  pl.BlockSpec((1, WIDTH), lambda i, j: (0, 0)),
                  pl.BlockSpec((N_HEADS, SGU_SPAN, SGU_SPAN), lambda i, j: (0, 0, 0)),
                  pl.BlockSpec((SGU_SPAN, N_HEADS), lambda i, j: (0, 0))],
        out_specs=pl.BlockSpec((1, SGU_BLOCK, WIDTH), lambda i, j: (i, j, 0)),
        out_shape=jax.ShapeDtypeStruct((b, s, WIDTH), BF16),
        compiler_params=_params("parallel", "parallel"),
        name="spatial_gating",
    )(uv, uv, ln_g, ln_b, w_s, b_s_t)


GDN_PAIR = 2 * CHUNK
GDN_HALO = 8
GDN_PREP_PAIRS = 2


def _gdn_prep_kernel(x_ref, halo_ref, cw_ref, aux_ref, gcrow_ref, u_ref, w_ref, qd_ref, kdt_ref, qk_ref,
                     pad_ref):
    first = pl.program_id(1) == 0
    halo = halo_ref[0]
    pad_ref[0:GDN_HALO, :] = jnp.where(first, jnp.zeros_like(halo), halo)
    pad_ref[GDN_HALO:, :] = x_ref[0]

    ri = lax.broadcasted_iota(jnp.int32, (GDN_PAIR, GDN_PAIR), 0)
    ci = lax.broadcasted_iota(jnp.int32, (GDN_PAIR, GDN_PAIR), 1)
    same_chunk = (ri // CHUNK) == (ci // CHUNK)
    causal = same_chunk & (ri >= ci)
    strict = same_chunk & (ri > ci)
    row = lax.broadcasted_iota(jnp.int32, (GDN_PAIR, 1), 0)

    for pair in range(GDN_PREP_PAIRS):
        _gdn_prep_pair(pair, pad_ref, cw_ref, aux_ref, gcrow_ref, causal, strict, row,
                       u_ref, w_ref, qd_ref, kdt_ref, qk_ref)


def _gdn_prep_pair(pair, pad_ref, cw_ref, aux_ref, gcrow_ref, causal, strict, row,
                   u_ref, w_ref, qd_ref, kdt_ref, qk_ref):
    rows = slice(pair * GDN_PAIR, (pair + 1) * GDN_PAIR)
    base = GDN_HALO + pair * GDN_PAIR

    def conv_silu(cols):
        y = cw_ref[GDN_CONV - 1:GDN_CONV, cols] * pad_ref[base:base + GDN_PAIR, cols]
        for j in range(GDN_CONV - 1):
            off = base - (GDN_CONV - 1) + j
            y = y + cw_ref[j:j + 1, cols] * pad_ref[off:off + GDN_PAIR, cols]
        return y * _sigmoid(y)

    def l2n(t):
        return t * lax.rsqrt(jnp.sum(t * t, axis=-1, keepdims=True) + RMS_EPS)

    aux = aux_ref[0, rows, :]
    heads = range(N_HEADS)
    head_cols = {h: slice(h * HEAD_DIM, (h + 1) * HEAD_DIM) for h in heads}
    ys, ps, rhss = {}, {}, {}
    for h in heads:
        cols = head_cols[h]
        q = l2n(conv_silu(slice(h * HEAD_DIM, (h + 1) * HEAD_DIM))) * (HEAD_DIM ** -0.5)
        k = l2n(conv_silu(slice(WIDTH + h * HEAD_DIM, WIDTH + (h + 1) * HEAD_DIM)))
        v = conv_silu(slice(2 * WIDTH + h * HEAD_DIM, 2 * WIDTH + (h + 1) * HEAD_DIM))
        gc = aux[:, AUX_GC + h:AUX_GC + h + 1]
        beta = aux[:, AUX_BETA + h:AUX_BETA + h + 1]
        gc_r = gcrow_ref[0, h:h + 1, rows]
        decay = jnp.exp(jnp.where(causal, gc - gc_r, NEG_INF))
        kb = k * beta
        kbf = k.astype(BF16)
        prod = _dot_nt(jnp.concatenate([kb, q], axis=0).astype(BF16), kbf)
        a = jnp.where(strict, prod[:GDN_PAIR] * decay, 0.0)
        qk_ref[0, rows, cols] = (prod[GDN_PAIR:] * decay).astype(BF16)
        egc = jnp.exp(gc)
        rhss[h] = jnp.concatenate([v * beta, kb * egc], axis=1)
        qd_ref[0, rows, cols] = (q * egc).astype(BF16)
        g_last = jnp.where(row < CHUNK, gc[CHUNK - 1:CHUNK, :], gc[GDN_PAIR - 1:GDN_PAIR, :])
        kdt_ref[0, cols, rows] = (k * jnp.exp(g_last - gc)).T.astype(BF16)
        ab = a.astype(BF16)
        ys[h] = -a
        ps[h] = _dot(ab, ab)

    n = 2
    while True:
        pbs = {h: ps[h].astype(BF16) for h in heads}
        ys = {h: ys[h] + ps[h] + _dot(ys[h].astype(BF16), pbs[h]) for h in heads}
        n *= 2
        if n >= CHUNK:
            break
        ps = {h: _dot(pbs[h], pbs[h]) for h in heads}

    for h in heads:
        sol = rhss[h] + _dot(ys[h].astype(BF16), rhss[h].astype(BF16))
        u_ref[0, rows, head_cols[h]] = sol[:, :HEAD_DIM]
        w_ref[0, rows, head_cols[h]] = sol[:, HEAD_DIM:].astype(BF16)


def gdn_prepare(qkv, conv_w, aux, gc_row):
    b, s, _ = qkv.shape
    blk = GDN_PREP_PAIRS * GDN_PAIR
    halo_blocks = blk // GDN_HALO
    tok = pl.BlockSpec((1, blk, WIDTH), lambda i, t: (i, t, 0))
    wide = lambda dt: jax.ShapeDtypeStruct((b, s, WIDTH), dt)
    return pl.pallas_call(
        _gdn_prep_kernel,
        grid=(b, s // blk),
        in_specs=[pl.BlockSpec((1, blk, 3 * WIDTH), lambda i, t: (i, t, 0)),
                  pl.BlockSpec((1, GDN_HALO, 3 * WIDTH),
                               lambda i, t: (i, jnp.maximum(t * halo_blocks - 1, 0), 0)),
                  pl.BlockSpec((GDN_CONV, 3 * WIDTH), lambda i, t: (0, 0)),
                  pl.BlockSpec((1, blk, AUX_LANES), lambda i, t: (i, t, 0)),
                  pl.BlockSpec((1, N_HEADS, blk), lambda i, t: (i, 0, t))],
        out_specs=[tok, tok, tok, pl.BlockSpec((1, WIDTH, blk), lambda i, t: (i, 0, t)), tok],
        out_shape=[wide(F32), wide(BF16), wide(BF16), jax.ShapeDtypeStruct((b, WIDTH, s), BF16), wide(BF16)],
        scratch_shapes=[pltpu.VMEM((GDN_HALO + blk, 3 * WIDTH), F32)],
        compiler_params=_params("parallel", "parallel"),
        name="gdn_prepare",
    )(qkv, qkv, conv_w, aux, gc_row)


SCAN_BLOCK = 256


def _gdn_scan_kernel(u_ref, w_ref, qd_ref, qk_ref, kdt_ref, aux_ref, gate_ref, ng_ref, o_ref, state_ref):
    @pl.when(pl.program_id(1) == 0)
    def _():
        state_ref[...] = jnp.zeros_like(state_ref)

    zeros = jnp.zeros((CHUNK, HEAD_DIM), BF16)
    for c in range(SCAN_BLOCK // CHUNK):
        rows = slice(c * CHUNK, (c + 1) * CHUNK)
        pair = slice((c // 2) * GDN_PAIR, (c // 2 + 1) * GDN_PAIR)
        last = (c + 1) * CHUNK - 1
        heads = range(N_HEADS)
        head_cols = [slice(h * HEAD_DIM, (h + 1) * HEAD_DIM) for h in heads]
        states = [state_ref[h] for h in heads]
        r1 = [_dot(jnp.concatenate([w_ref[0, rows, head_cols[h]], qd_ref[0, rows, head_cols[h]]], axis=0),
                   states[h].astype(BF16)) for h in heads]
        r2 = []
        for h in heads:
            v_new = (u_ref[0, rows, head_cols[h]] - r1[h][:CHUNK]).astype(BF16)
            v_pad = jnp.concatenate([v_new, zeros] if c % 2 == 0 else [zeros, v_new], axis=0)
            r2.append(_dot(jnp.concatenate([qk_ref[0, rows, head_cols[h]], kdt_ref[0, head_cols[h], pair]],
                                           axis=0), v_pad))
        for h in heads:
            decay_last = jnp.exp(aux_ref[0, last:last + 1, AUX_GC + h:AUX_GC + h + 1])
            state_ref[h] = states[h] * decay_last + r2[h][CHUNK:]
        for h in heads:
            o = r1[h][CHUNK:] + r2[h][:CHUNK]
            o = o * lax.rsqrt(jnp.mean(o * o, axis=-1, keepdims=True) + RMS_EPS) * ng_ref[...]
            gt = gate_ref[0, rows, head_cols[h]]
            o_ref[0, rows, head_cols[h]] = (o * (gt * _sigmoid(gt))).astype(o_ref.dtype)


def gdn_scan(u, w, qd, qk, kdt, aux, gate, norm_g):
    b, s, _ = u.shape
    tok = pl.BlockSpec((1, SCAN_BLOCK, WIDTH), lambda i, t: (i, t, 0))
    return pl.pallas_call(
        _gdn_scan_kernel,
        grid=(b, s // SCAN_BLOCK),
        in_specs=[tok, tok, tok, tok,
                  pl.BlockSpec((1, WIDTH, SCAN_BLOCK), lambda i, t: (i, 0, t)),
                  pl.BlockSpec((1, SCAN_BLOCK, AUX_LANES), lambda i, t: (i, t, 0)),
                  tok, pl.BlockSpec((1, HEAD_DIM), lambda i, t: (0, 0))],
        out_specs=tok,
        out_shape=jax.ShapeDtypeStruct((b, s, WIDTH), BF16),
        scratch_shapes=[pltpu.VMEM((N_HEADS, HEAD_DIM, HEAD_DIM), F32)],
        compiler_params=_params("parallel", "arbitrary"),
        name="gdn_scan",
    )(u, w, qd, qk, kdt, aux, gate, norm_g)


def gdn_mixer(qkv, conv_w, aux, gc_row, gate, norm_g):
    u, w, qd, kdt, qk = gdn_prepare(qkv, conv_w, aux, gc_row)
    return gdn_scan(u, w, qd, qk, kdt, aux, gate, norm_g)


MERGE_TM = 512
MERGE_TN = 512


def _merge_kernel(xb_ref, ya_ref, yb_ref, yc_ref, wg0, wg1, wg2, bg0, bg1, bg2, wa, wb, wc, o_ref):
    xb = xb_ref[...]

    def branch(y_ref, wp, wg, bg):
        return _sigmoid(_dot(xb, wg[...]) + bg[...]) * _dot(y_ref[...], wp[...])

    merged = branch(ya_ref, wa, wg0, bg0) + branch(yb_ref, wb, wg1, bg1) + branch(yc_ref, wc, wg2, bg2)
    o_ref[...] = merged.astype(o_ref.dtype)


def merge_branches(xb, ya, yb, yc, wg, bg, wp):
    m = xb.shape[0]
    row = lambda width: pl.BlockSpec((MERGE_TM, width), lambda i, j: (i, 0))
    col = lambda depth: pl.BlockSpec((depth, MERGE_TN), lambda i, j: (0, j))
    return pl.pallas_call(
        _merge_kernel,
        grid=(m // MERGE_TM, D_MODEL // MERGE_TN),
        in_specs=[row(D_MODEL), row(WIDTH), row(WIDTH), row(WIDTH),
                  col(D_MODEL), col(D_MODEL), col(D_MODEL), col(1), col(1), col(1),
                  col(WIDTH), col(WIDTH), col(WIDTH)],
        out_specs=pl.BlockSpec((MERGE_TM, MERGE_TN), lambda i, j: (i, j)),
        out_shape=jax.ShapeDtypeStruct((m, D_MODEL), BF16),
        compiler_params=_params("parallel", "arbitrary"),
        name="merge_branches",
    )(xb, ya, yb, yc, *wg, *bg, *wp)


OUT_TM = 256


def _proj_ln_kernel(y_ref, w_ref, x_ref, g_ref, b_ref, o_ref, ob_ref):
    r = DEEPNORM_ALPHA * x_ref[...] + _dot(y_ref[...], w_ref[...])
    out = _layer_norm_rows(r, g_ref[...], b_ref[...])
    o_ref[...] = out
    ob_ref[...] = out.astype(BF16)


def proj_residual_ln(y, w, x, g, b):
    m, k = y.shape
    vec = pl.BlockSpec((1, D_MODEL), lambda i: (0, 0))
    row = pl.BlockSpec((OUT_TM, D_MODEL), lambda i: (i, 0))
    return pl.pallas_call(
        _proj_ln_kernel,
        grid=(m // OUT_TM,),
        in_specs=[pl.BlockSpec((OUT_TM, k), lambda i: (i, 0)),
                  pl.BlockSpec((k, D_MODEL), lambda i: (0, 0)),
                  row, vec, vec],
        out_specs=[row, row],
        out_shape=[jax.ShapeDtypeStruct((m, D_MODEL), F32), jax.ShapeDtypeStruct((m, D_MODEL), BF16)],
        compiler_params=_params("parallel"),
        name="proj_residual_ln",
    )(y, w, x, g, b)


DOWN_TM = 512
DOWN_TK = D_FF_PAD // 4


def _ffn_down_kernel(a_ref, w_ref, x_ref, g_ref, b_ref, o_ref, ob_ref, acc_ref):
    kk = pl.program_id(1)
    last = pl.num_programs(1) - 1

    @pl.when(kk == 0)
    def _():
        acc_ref[...] = DEEPNORM_ALPHA * x_ref[...] + _dot(a_ref[...], w_ref[...])

    @pl.when((kk > 0) & (kk < last))
    def _():
        acc_ref[...] += _dot(a_ref[...], w_ref[...])

    @pl.when(kk == last)
    def _():
        out = _layer_norm_rows(acc_ref[...] + _dot(a_ref[...], w_ref[...]), g_ref[...], b_ref[...])
        o_ref[...] = out
        ob_ref[...] = out.astype(BF16)


def ffn_down_ln(act, w, x, g, b):
    m, k = act.shape
    assert k // DOWN_TK >= 2
    vec = pl.BlockSpec((1, D_MODEL), lambda i, j: (0, 0))
    row = pl.BlockSpec((DOWN_TM, D_MODEL), lambda i, j: (i, 0))
    return pl.pallas_call(
        _ffn_down_kernel,
        grid=(m // DOWN_TM, k // DOWN_TK),
        in_specs=[pl.BlockSpec((DOWN_TM, DOWN_TK), lambda i, j: (i, j)),
                  pl.BlockSpec((DOWN_TK, D_MODEL), lambda i, j: (j, 0)),
                  row, vec, vec],
        out_specs=[row, row],
        out_shape=[jax.ShapeDtypeStruct((m, D_MODEL), F32), jax.ShapeDtypeStruct((m, D_MODEL), BF16)],
        scratch_shapes=[pltpu.VMEM((DOWN_TM, D_MODEL), F32)],
        compiler_params=_params("parallel", "arbitrary"),
        name="ffn_down_ln",
    )(act, w, x, g, b)


FFN_TM = 1024
FFN_TN = 512
FFN_HALO = 16


def _stage_rows_with_halo(x_ref, halo_ref, xs_ref, first):
    halo = halo_ref[...]
    xs_ref[0:FFN_HALO, :] = jnp.where(first, jnp.zeros_like(halo), halo)
    xs_ref[FFN_HALO:, :] = x_ref[...]


def _causal_taps(r, cw, taps):
    tm = r.shape[0] - FFN_HALO
    y = cw[taps - 1:taps] * r[FFN_HALO:FFN_HALO + tm]
    for j in range(taps - 1):
        off = FFN_HALO - (taps - 1) + j
        y = y + cw[j:j + 1] * r[off:off + tm]
    return y


def _ffn_up_kernel(x_ref, halo_ref, wg_ref, wv_ref, cwg_ref, cwv_ref, cbg_ref, cbv_ref, o_ref, xs_ref,
                   *, tiles_per_seq):
    @pl.when(pl.program_id(1) == 0)
    def _():
        _stage_rows_with_halo(x_ref, halo_ref, xs_ref, (pl.program_id(0) % tiles_per_seq) == 0)

    xs = xs_ref[...]
    hg = cbg_ref[...] + _causal_taps(_dot(xs, wg_ref[...]), cwg_ref[...], FFN_CONV)
    hv = cbv_ref[...] + _causal_taps(_dot(xs, wv_ref[...]), cwv_ref[...], FFN_CONV)
    o_ref[...] = (hg * _sigmoid(hg) * hv).astype(o_ref.dtype)


def ffn_up(xb, seq_len, wg, wv, cwg, cwv, cbg, cbv):
    m, k = xb.shape
    n = wg.shape[1]
    assert seq_len % FFN_TM == 0
    halo_blocks = FFN_TM // FFN_HALO
    col = lambda depth: pl.BlockSpec((depth, FFN_TN), lambda i, j: (0, j))
    return pl.pallas_call(
        functools.partial(_ffn_up_kernel, tiles_per_seq=seq_len // FFN_TM),
        grid=(m // FFN_TM, n // FFN_TN),
        in_specs=[pl.BlockSpec((FFN_TM, k), lambda i, j: (i, 0)),
                  pl.BlockSpec((FFN_HALO, k), lambda i, j: (jnp.maximum(i * halo_blocks - 1, 0), 0)),
                  col(k), col(k), col(FFN_CONV), col(FFN_CONV), col(1), col(1)],
        out_specs=pl.BlockSpec((FFN_TM, FFN_TN), lambda i, j: (i, j)),
        out_shape=jax.ShapeDtypeStruct((m, n), BF16),
        scratch_shapes=[pltpu.VMEM((FFN_HALO + FFN_TM, k), BF16)],
        compiler_params=_params("parallel", "arbitrary"),
        name="ffn_up",
    )(xb, xb, wg, wv, cwg, cwv, cbg, cbv)


_FOX_QKV = (0, 3 * WIDTH)
_FOX_F = (_FOX_QKV[1], _FOX_QKV[1] + N_HEADS)
_SGU_UV = (_FOX_F[1], _FOX_F[1] + 2 * WIDTH)
_GDN_QKV = (_SGU_UV[1], _SGU_UV[1] + 3 * WIDTH)
_GDN_A = (_GDN_QKV[1], _GDN_QKV[1] + N_HEADS)
_GDN_B = (_GDN_A[1], _GDN_A[1] + N_HEADS)
_GDN_GATE = (_GDN_B[1], _GDN_B[1] + WIDTH)
_GATES = (_GDN_GATE[1], _GDN_GATE[1] + 3 * D_MODEL)


def _cols(w, span):
    return w[..., span[0]:span[1]]


def _pad_lanes(v, offset):
    return jnp.zeros((1, AUX_LANES), F32).at[0, offset:offset + v.shape[0]].set(v)


def _layer(x, xb, bsz, seq, w_in, b_in, sgu_ln_g, sgu_ln_b, sgu_w, sgu_b, gdn_conv_w, gdn_a_log,
           gdn_dt_bias, gdn_norm_g, w_proj_a, w_proj_b, w_proj_c, w_out, ln1_g, ln1_b,
           ffn_w_up, ffn_conv_w, ffn_conv_b, ffn_w_down, ln2_g, ln2_b):
    tokens = bsz * seq
    wb = lambda span: _cols(w_in, span).astype(BF16)
    bias = lambda span: _cols(b_in, span)[None, :]

    fox_q_t, fox_k, fox_v_t = fox_in_proj(xb, bsz, seq, wb(_FOX_QKV), bias(_FOX_QKV))
    sgu_uv = matmul_bias(xb, wb(_SGU_UV), bias(_SGU_UV), F32, 1024, 1024, "in_sgu")
    gdn_qkv = matmul_bias(xb, wb(_GDN_QKV), bias(_GDN_QKV), F32, 1024, 1024, "in_gdn")
    pad = AUX_LANES - 3 * N_HEADS
    scalar_spans = (_GDN_GATE, _FOX_F, _GDN_A, _GDN_B)
    w_gs = jnp.concatenate([_cols(w_in, sp) for sp in scalar_spans] + [jnp.zeros((D_MODEL, pad), F32)],
                           axis=1).astype(BF16)
    b_gs = jnp.concatenate([_cols(b_in, sp) for sp in scalar_spans] + [jnp.zeros((pad,), F32)])[None, :]
    gate_scalars = matmul_bias(xb, w_gs, b_gs, F32, 1024, WIDTH + AUX_LANES, "in_gate_scalars")
    gate_scalars = gate_scalars.reshape(bsz, seq, WIDTH + AUX_LANES)

    aux, aux_t = aux_scalars(gate_scalars, _pad_lanes(gdn_a_log, AUX_GC), _pad_lanes(gdn_dt_bias, AUX_GC))
    c_row = aux_t[:, AUX_C:AUX_C + N_HEADS, :].reshape(bsz * N_HEADS, seq // FOX_T, FOX_T)
    gc_row = aux_t[:, AUX_GC:AUX_GC + N_HEADS, :]

    y_a = fox_attention(fox_q_t, fox_k.reshape(bsz, seq, WIDTH), fox_v_t, aux, c_row)
    y_b = spatial_gating(sgu_uv.reshape(bsz, seq, 2 * WIDTH), sgu_ln_g[None, :], sgu_ln_b[None, :],
                         sgu_w, sgu_b.T)
    y_c = gdn_mixer(gdn_qkv.reshape(bsz, seq, 3 * WIDTH), gdn_conv_w, aux, gc_row,
                    gate_scalars, gdn_norm_g[None, :])

    g0 = _GATES[0]
    wg = [w_in[:, g0 + i * D_MODEL:g0 + (i + 1) * D_MODEL].astype(BF16) for i in range(3)]
    bg = [b_in[None, g0 + i * D_MODEL:g0 + (i + 1) * D_MODEL] for i in range(3)]
    wp = [w_proj_a.astype(BF16), w_proj_b.astype(BF16), w_proj_c.astype(BF16)]
    merged = merge_branches(xb, y_a.reshape(tokens, WIDTH), y_b.reshape(tokens, WIDTH),
                            y_c.reshape(tokens, WIDTH), wg, bg, wp)
    x, xb = proj_residual_ln(merged, w_out.astype(BF16), x, ln1_g[None, :], ln1_b[None, :])

    ff_pad = D_FF_PAD - D_FF
    pad_cols = lambda t: jnp.pad(t, ((0, 0), (0, ff_pad)))
    act = ffn_up(xb, seq,
                 pad_cols(ffn_w_up[:, :D_FF].astype(BF16)), pad_cols(ffn_w_up[:, D_FF:].astype(BF16)),
                 pad_cols(ffn_conv_w[:, :D_FF]), pad_cols(ffn_conv_w[:, D_FF:]),
                 pad_cols(ffn_conv_b[None, :D_FF]), pad_cols(ffn_conv_b[None, D_FF:]))
    w_down = jnp.pad(ffn_w_down.astype(BF16), ((0, ff_pad), (0, 0)))
    return ffn_down_ln(act, w_down, x, ln2_g[None, :], ln2_b[None, :])


@jax.jit
def kernel(x, w_in, b_in, sgu_ln_g, sgu_ln_b, sgu_w, sgu_b, gdn_conv_w, gdn_a_log, gdn_dt_bias, gdn_norm_g, w_proj_a, w_proj_b, w_proj_c, w_out, ln1_g, ln1_b, ffn_w_up, ffn_conv_w, ffn_conv_b, ffn_w_down, ln2_g, ln2_b):
    bsz, seq, _ = x.shape
    stacked = (w_in, b_in, sgu_ln_g, sgu_ln_b, sgu_w, sgu_b, gdn_conv_w, gdn_a_log, gdn_dt_bias,
               gdn_norm_g, w_proj_a, w_proj_b, w_proj_c, w_out, ln1_g, ln1_b, ffn_w_up, ffn_conv_w,
               ffn_conv_b, ffn_w_down, ln2_g, ln2_b)
    x = x.reshape(bsz * seq, D_MODEL)
    xb = x.astype(BF16)
    for layer in range(w_in.shape[0]):
        x, xb = _layer(x, xb, bsz, seq, *(t[layer] for t in stacked))
    return x.reshape(bsz, seq, D_MODEL)
```

```python
import functools

import jax
import jax.numpy as jnp
from jax import lax
from jax.experimental import pallas as pl
from jax.experimental.pallas import tpu as pltpu

F32 = jnp.float32
BF16 = jnp.bfloat16

D_MODEL = 2048
HEAD_DIM = 128
N_HEADS = 8
WIDTH = N_HEADS * HEAD_DIM
CHUNK = 64
SGU_SPAN = 128
GDN_CONV = 4
D_FF = 5504
D_FF_PAD = 5632
FFN_CONV = 3
DEPTH = 4
DEEPNORM_ALPHA = (2 * DEPTH) ** 0.25
LN_EPS = 1e-5
RMS_EPS = 1e-6

AUX_LANES = 128
AUX_C = 0
AUX_GC = 8
AUX_BETA = 16

VMEM_LIMIT = 56 * 1024 * 1024

NEG_INF = float("-inf")


def _params(*semantics):
    return pltpu.CompilerParams(dimension_semantics=semantics, vmem_limit_bytes=VMEM_LIMIT)


def _dot(a, b):
    return jnp.dot(a, b, preferred_element_type=F32)


def _dot_nt(a, b):
    return lax.dot_general(a, b, (((1,), (1,)), ((), ())), preferred_element_type=F32)


def _split3(x):
    hi = x.astype(BF16)
    r1 = x - hi.astype(F32)
    mid = r1.astype(BF16)
    lo = (r1 - mid.astype(F32)).astype(BF16)
    return hi, mid, lo


def _softplus(z):
    return jnp.maximum(z, 0.0) + jnp.log1p(jnp.exp(-jnp.abs(z)))


def _sigmoid(z):
    return 1.0 / (1.0 + jnp.exp(-z))


def _layer_norm_rows(y, g, b):
    mu = jnp.mean(y, axis=-1, keepdims=True)
    yc = y - mu
    var = jnp.mean(yc * yc, axis=-1, keepdims=True)
    return yc * lax.rsqrt(var + LN_EPS) * g + b


def _mm_bias_kernel(x_ref, w_ref, b_ref, o_ref):
    acc = _dot(x_ref[...], w_ref[...])
    o_ref[...] = (acc + b_ref[...]).astype(o_ref.dtype)


def matmul_bias(xb, w, b, out_dtype, tm, tn, name):
    m, k = xb.shape
    n = w.shape[1]
    return pl.pallas_call(
        _mm_bias_kernel,
        grid=(m // tm, n // tn),
        in_specs=[pl.BlockSpec((tm, k), lambda i, j: (i, 0)),
                  pl.BlockSpec((k, tn), lambda i, j: (0, j)),
                  pl.BlockSpec((1, tn), lambda i, j: (0, j))],
        out_specs=pl.BlockSpec((tm, tn), lambda i, j: (i, j)),
        out_shape=jax.ShapeDtypeStruct((m, n), out_dtype),
        compiler_params=_params("parallel", "arbitrary"),
        name=name,
    )(xb, w, b)


AUX_BLOCK = 256


def _aux_kernel(z_ref, alog_ref, dt_ref, o_ref, ot_ref, carry_ref):
    @pl.when(pl.program_id(1) == 0)
    def _():
        carry_ref[...] = jnp.zeros_like(carry_ref)

    z = z_ref[0]
    lane = lax.broadcasted_iota(jnp.int32, z.shape, 1)
    log_f = -_softplus(-z)
    g = -jnp.exp(alog_ref[...]) * _softplus(z + dt_ref[...])
    beta = _sigmoid(z)

    ri = lax.broadcasted_iota(jnp.int32, (AUX_BLOCK, AUX_BLOCK), 0)
    ci = lax.broadcasted_iota(jnp.int32, (AUX_BLOCK, AUX_BLOCK), 1)
    tri = ri >= ci
    ones_seq = jnp.where(tri, 1.0, 0.0).astype(BF16)
    ones_chunk = jnp.where(tri & ((ri // CHUNK) == (ci // CHUNK)), 1.0, 0.0).astype(BF16)

    def cumsum(ones, val):
        hi, mid, lo = _split3(val)
        return _dot(ones, hi) + _dot(ones, mid) + _dot(ones, lo)

    c = cumsum(ones_seq, log_f) + carry_ref[...]
    carry_ref[...] = c[AUX_BLOCK - 1:AUX_BLOCK, :]
    gc = cumsum(ones_chunk, g)
    out = jnp.where(lane < AUX_GC, c, jnp.where(lane < AUX_BETA, gc, beta))
    o_ref[0] = out
    ot_ref[0] = out.T


def aux_scalars(z, alog_pad, dt_pad):
    b, s, n = z.shape
    last = n // AUX_LANES - 1
    return pl.pallas_call(
        _aux_kernel,
        grid=(b, s // AUX_BLOCK),
        in_specs=[pl.BlockSpec((1, AUX_BLOCK, AUX_LANES), lambda i, j: (i, j, last)),
                  pl.BlockSpec((1, AUX_LANES), lambda i, j: (0, 0)),
                  pl.BlockSpec((1, AUX_LANES), lambda i, j: (0, 0))],
        out_specs=[pl.BlockSpec((1, AUX_BLOCK, AUX_LANES), lambda i, j: (i, j, 0)),
                   pl.BlockSpec((1, AUX_LANES, AUX_BLOCK), lambda i, j: (i, 0, j))],
        out_shape=[jax.ShapeDtypeStruct((b, s, AUX_LANES), F32),
                   jax.ShapeDtypeStruct((b, AUX_LANES, s), F32)],
        scratch_shapes=[pltpu.VMEM((1, AUX_LANES), F32)],
        compiler_params=_params("parallel", "arbitrary"),
        name="aux_scalars",
    )(z, alog_pad, dt_pad)


def _pick_lane(x, idx):
    lane = lax.broadcasted_iota(jnp.int32, x.shape, 1)
    return jnp.sum(jnp.where(lane == idx, x, 0.0), axis=1, keepdims=True)


LOG2E = 1.4426950408889634
FOX_T = 512
FOX_QSCALE = LOG2E * HEAD_DIM ** -0.5
FOX_BIAS_TERMS = 3
FOX_GROUP = 4


def _fox_kernel(qt_ref, k_ref, vt_ref, aux_ref, crow_ref, o_ref, kaug_ref):
    hg = pl.program_id(1)
    qi = pl.program_id(2)
    nk = kaug_ref.shape[1] // FOX_T
    group = range(FOX_GROUP)

    @pl.when(qi == 0)
    def _():
        lane = lax.broadcasted_iota(jnp.int32, (FOX_T, HEAD_DIM), 1)
        for g in group:
            for j in range(nk):
                rows = slice(j * FOX_T, (j + 1) * FOX_T)
                ck = -LOG2E * _pick_lane(aux_ref[0, rows, :], AUX_C + hg * FOX_GROUP + g)
                pieces = [t.astype(F32) for t in _split3(ck)]
                bias = jnp.zeros((FOX_T, HEAD_DIM), F32)
                for i in reversed(range(FOX_BIAS_TERMS)):
                    bias = jnp.where(lane == i, pieces[i], bias)
                kaug_ref[g, rows, 0:HEAD_DIM] = k_ref[0, rows, g * HEAD_DIM:(g + 1) * HEAD_DIM]
                kaug_ref[g, rows, HEAD_DIM:2 * HEAD_DIM] = bias.astype(BF16)

    sub = lax.broadcasted_iota(jnp.int32, (HEAD_DIM, FOX_T), 0)
    ones = jnp.where(sub < FOX_BIAS_TERMS, 1.0, 0.0).astype(BF16)
    q_aug = [jnp.concatenate([qt_ref[0, g, 0], ones], axis=0) for g in group]
    cq = [LOG2E * crow_ref[g, pl.ds(qi, 1), :] for g in group]

    def block(j, carry, diagonal):
        start = pl.multiple_of(j * FOX_T, FOX_T)
        scores = [_dot(kaug_ref[g, pl.ds(start, FOX_T), :], q_aug[g]) for g in group]
        stats, probs = [], []
        for g in group:
            m, l, _ = carry[g]
            s = scores[g]
            if diagonal:
                key = lax.broadcasted_iota(jnp.int32, s.shape, 0)
                qry = lax.broadcasted_iota(jnp.int32, s.shape, 1)
                s = jnp.where(qry >= key, s, NEG_INF)
            m_new = jnp.maximum(m, jnp.max(s, axis=0, keepdims=True) + cq[g])
            alpha = jnp.exp2(m - m_new)
            p = jnp.exp2(s + (cq[g] - m_new))
            stats.append((m_new, alpha * l + jnp.sum(p, axis=0, keepdims=True), alpha))
            probs.append(p.astype(BF16))
        out = []
        for g in group:
            m_new, l, alpha = stats[g]
            acc = alpha * carry[g][2] + _dot(vt_ref[0, g, j], probs[g])
            out.append((m_new, l, acc))
        return tuple(out)

    init = tuple((jnp.full((1, FOX_T), NEG_INF, F32), jnp.zeros((1, FOX_T), F32),
                  jnp.zeros((HEAD_DIM, FOX_T), F32)) for _ in group)
    carry = lax.fori_loop(0, qi, lambda j, c: block(j, c, False), init)
    carry = block(qi, carry, True)
    for g in group:
        _, l, acc = carry[g]
        o_ref[0, :, g * HEAD_DIM:(g + 1) * HEAD_DIM] = (acc / l).T.astype(o_ref.dtype)


FOX_IN_TM = 1024


def _fox_in_kernel(x_ref, w_ref, b_ref, qt_ref, k_ref, vt_ref):
    j = pl.program_id(1)

    def projection():
        return _dot(x_ref[...], w_ref[...]) + b_ref[...]

    def store_transposed(dst_ref, y):
        for t in range(FOX_IN_TM // FOX_T):
            for h in range(N_HEADS):
                tile = y[t * FOX_T:(t + 1) * FOX_T, h * HEAD_DIM:(h + 1) * HEAD_DIM]
                dst_ref[0, h, t] = tile.T.astype(BF16)

    @pl.when(j == 0)
    def _():
        store_transposed(qt_ref, projection() * FOX_QSCALE)

    @pl.when(j == 1)
    def _():
        k_ref[...] = projection().astype(BF16)

    @pl.when(j == 2)
    def _():
        store_transposed(vt_ref, projection())


def fox_in_proj(xb, bsz, seq, w, b):
    m, k = xb.shape
    assert seq % FOX_IN_TM == 0
    per_seq = seq // FOX_IN_TM
    tiles = FOX_IN_TM // FOX_T
    t_spec = pl.BlockSpec((1, N_HEADS, tiles, HEAD_DIM, FOX_T),
                          lambda i, j: (i // per_seq, 0, i % per_seq, 0, 0))
    t_shape = jax.ShapeDtypeStruct((bsz, N_HEADS, seq // FOX_T, HEAD_DIM, FOX_T), BF16)
    return pl.pallas_call(
        _fox_in_kernel,
        grid=(m // FOX_IN_TM, 3),
        in_specs=[pl.BlockSpec((FOX_IN_TM, k), lambda i, j: (i, 0)),
                  pl.BlockSpec((k, WIDTH), lambda i, j: (0, j)),
                  pl.BlockSpec((1, WIDTH), lambda i, j: (0, j))],
        out_specs=[t_spec, pl.BlockSpec((FOX_IN_TM, WIDTH), lambda i, j: (i, 0)), t_spec],
        out_shape=[t_shape, jax.ShapeDtypeStruct((m, WIDTH), BF16), t_shape],
        compiler_params=_params("parallel", "arbitrary"),
        name="in_fox",
    )(xb, w, b)


def fox_attention(q_t, k, v_t, aux, c_row):
    b, s, _ = k.shape
    nt = s // FOX_T
    groups = N_HEADS // FOX_GROUP

    return pl.pallas_call(
        _fox_kernel,
        grid=(b, groups, nt),
        in_specs=[pl.BlockSpec((1, FOX_GROUP, 1, HEAD_DIM, FOX_T), lambda i, h, t: (i, h, t, 0, 0)),
                  pl.BlockSpec((1, s, FOX_GROUP * HEAD_DIM), lambda i, h, t: (i, 0, h)),
                  pl.BlockSpec((1, FOX_GROUP, nt, HEAD_DIM, FOX_T), lambda i, h, t: (i, h, 0, 0, 0)),
                  pl.BlockSpec((1, s, AUX_LANES), lambda i, h, t: (i, 0, 0)),
                  pl.BlockSpec((FOX_GROUP, nt, FOX_T), lambda i, h, t: (i * groups + h, 0, 0))],
        out_specs=pl.BlockSpec((1, FOX_T, FOX_GROUP * HEAD_DIM), lambda i, h, t: (i, t, h)),
        out_shape=jax.ShapeDtypeStruct((b, s, WIDTH), BF16),
        scratch_shapes=[pltpu.VMEM((FOX_GROUP, s, 2 * HEAD_DIM), BF16)],
        compiler_params=_params("parallel", "parallel", "arbitrary"),
        name="fox_attention",
    )(q_t, k, v_t, aux, c_row)


SGU_BLOCK = 4 * SGU_SPAN


def _sgu_kernel(u_ref, v_ref, g_ref, b_ref, w_ref, bst_ref, o_ref):
    ri = lax.broadcasted_iota(jnp.int32, (SGU_SPAN, SGU_SPAN), 0)
    ci = lax.broadcasted_iota(jnp.int32, (SGU_SPAN, SGU_SPAN), 1)
    mask = (ri // CHUNK) >= (ci // CHUNK)
    for g in range(N_HEADS):
        cols = slice(g * HEAD_DIM, (g + 1) * HEAD_DIM)
        wm = jnp.where(mask, w_ref[g], 0.0).astype(BF16)
        spans = [slice(n * SGU_SPAN, (n + 1) * SGU_SPAN) for n in range(SGU_BLOCK // SGU_SPAN)]
        vns = [_layer_norm_rows(v_ref[0, rows, cols], g_ref[:, cols], b_ref[:, cols]).astype(BF16)
               for rows in spans]
        mixed = [_dot(wm, vn) + bst_ref[:, g:g + 1] for vn in vns]
        for rows, mx in zip(spans, mixed):
            o_ref[0, rows, cols] = (u_ref[0, rows, cols] * mx).astype(o_ref.dtype)


def spatial_gating(uv, ln_g, ln_b, w_s, b_s_t):
    b, s, _ = uv.shape
    return pl.pallas_call(
        _sgu_kernel,
        grid=(b, s // SGU_BLOCK),
        in_specs=[pl.BlockSpec((1, SGU_BLOCK, WIDTH), lambda i, j: (i, j, 0)),
                  pl.BlockSpec((1, SGU_BLOCK, WIDTH), lambda i, j: (i, j, 1)),
                  pl.BlockSpec((1, WIDTH), lambda i, j: (0, 0)),
                  pl.BlockSpec((1, WIDTH), lambda i, j: (0, 0)),
                  pl.BlockSpec((N_HEADS, SGU_SPAN, SGU_SPAN), lambda i, j: (0, 0, 0)),
                  pl.BlockSpec((SGU_SPAN, N_HEADS), lambda i, j: (0, 0))],
        out_specs=pl.BlockSpec((1, SGU_BLOCK, WIDTH), lambda i, j: (i, j, 0)),
        out_shape=jax.ShapeDtypeStruct((b, s, WIDTH), BF16),
        compiler_params=_params("parallel", "parallel"),
        name="spatial_gating",
    )(uv, uv, ln_g, ln_b, w_s, b_s_t)


GDN_PAIR = 2 * CHUNK
GDN_HALO = 8
GDN_PREP_PAIRS = 2


def _gdn_prep_kernel(x_ref, halo_ref, cw_ref, aux_ref, gcrow_ref, u_ref, w_ref, qd_ref, kdt_ref, qk_ref,
                     pad_ref):
    first = pl.program_id(1) == 0
    halo = halo_ref[0]
    pad_ref[0:GDN_HALO, :] = jnp.where(first, jnp.zeros_like(halo), halo)
    pad_ref[GDN_HALO:, :] = x_ref[0]

    ri = lax.broadcasted_iota(jnp.int32, (GDN_PAIR, GDN_PAIR), 0)
    ci = lax.broadcasted_iota(jnp.int32, (GDN_PAIR, GDN_PAIR), 1)
    same_chunk = (ri // CHUNK) == (ci // CHUNK)
    causal = same_chunk & (ri >= ci)
    strict = same_chunk & (ri > ci)
    row = lax.broadcasted_iota(jnp.int32, (GDN_PAIR, 1), 0)

    for pair in range(GDN_PREP_PAIRS):
        _gdn_prep_pair(pair, pad_ref, cw_ref, aux_ref, gcrow_ref, causal, strict, row,
                       u_ref, w_ref, qd_ref, kdt_ref, qk_ref)


def _gdn_prep_pair(pair, pad_ref, cw_ref, aux_ref, gcrow_ref, causal, strict, row,
                   u_ref, w_ref, qd_ref, kdt_ref, qk_ref):
    rows = slice(pair * GDN_PAIR, (pair + 1) * GDN_PAIR)
    base = GDN_HALO + pair * GDN_PAIR

    def conv_silu(cols):
        y = cw_ref[GDN_CONV - 1:GDN_CONV, cols] * pad_ref[base:base + GDN_PAIR, cols]
        for j in range(GDN_CONV - 1):
            off = base - (GDN_CONV - 1) + j
            y = y + cw_ref[j:j + 1, cols] * pad_ref[off:off + GDN_PAIR, cols]
        return y * _sigmoid(y)

    def l2n(t):
        return t * lax.rsqrt(jnp.sum(t * t, axis=-1, keepdims=True) + RMS_EPS)

    aux = aux_ref[0, rows, :]
    heads = range(N_HEADS)
    head_cols = {h: slice(h * HEAD_DIM, (h + 1) * HEAD_DIM) for h in heads}
    ys, ps, rhss = {}, {}, {}
    for h in heads:
        cols = head_cols[h]
        q = l2n(conv_silu(slice(h * HEAD_DIM, (h + 1) * HEAD_DIM))) * (HEAD_DIM ** -0.5)
        k = l2n(conv_silu(slice(WIDTH + h * HEAD_DIM, WIDTH + (h + 1) * HEAD_DIM)))
        v = conv_silu(slice(2 * WIDTH + h * HEAD_DIM, 2 * WIDTH + (h + 1) * HEAD_DIM))
        gc = aux[:, AUX_GC + h:AUX_GC + h + 1]
        beta = aux[:, AUX_BETA + h:AUX_BETA + h + 1]
        gc_r = gcrow_ref[0, h:h + 1, rows]
        decay = jnp.exp(jnp.where(causal, gc - gc_r, NEG_INF))
        kb = k * beta
        kbf = k.astype(BF16)
        prod = _dot_nt(jnp.concatenate([kb, q], axis=0).astype(BF16), kbf)
        a = jnp.where(strict, prod[:GDN_PAIR] * decay, 0.0)
        qk_ref[0, rows, cols] = (prod[GDN_PAIR:] * decay).astype(BF16)
        egc = jnp.exp(gc)
        rhss[h] = jnp.concatenate([v * beta, kb * egc], axis=1)
        qd_ref[0, rows, cols] = (q * egc).astype(BF16)
        g_last = jnp.where(row < CHUNK, gc[CHUNK - 1:CHUNK, :], gc[GDN_PAIR - 1:GDN_PAIR, :])
        kdt_ref[0, cols, rows] = (k * jnp.exp(g_last - gc)).T.astype(BF16)
        ab = a.astype(BF16)
        ys[h] = -a
        ps[h] = _dot(ab, ab)

    n = 2
    while True:
        pbs = {h: ps[h].astype(BF16) for h in heads}
        ys = {h: ys[h] + ps[h] + _dot(ys[h].astype(BF16), pbs[h]) for h in heads}
        n *= 2
        if n >= CHUNK:
            break
        ps = {h: _dot(pbs[h], pbs[h]) for h in heads}

    for h in heads:
        sol = rhss[h] + _dot(ys[h].astype(BF16), rhss[h].astype(BF16))
        u_ref[0, rows, head_cols[h]] = sol[:, :HEAD_DIM]
        w_ref[0, rows, head_cols[h]] = sol[:, HEAD_DIM:].astype(BF16)


def gdn_prepare(qkv, conv_w, aux, gc_row):
    b, s, _ = qkv.shape
    blk = GDN_PREP_PAIRS * GDN_PAIR
    halo_blocks = blk // GDN_HALO
    tok = pl.BlockSpec((1, blk, WIDTH), lambda i, t: (i, t, 0))
    wide = lambda dt: jax.ShapeDtypeStruct((b, s, WIDTH), dt)
    return pl.pallas_call(
        _gdn_prep_kernel,
        grid=(b, s // blk),
        in_specs=[pl.BlockSpec((1, blk, 3 * WIDTH), lambda i, t: (i, t, 0)),
                  pl.BlockSpec((1, GDN_HALO, 3 * WIDTH),
                               lambda i, t: (i, jnp.maximum(t * halo_blocks - 1, 0), 0)),
                  pl.BlockSpec((GDN_CONV, 3 * WIDTH), lambda i, t: (0, 0)),
                  pl.BlockSpec((1, blk, AUX_LANES), lambda i, t: (i, t, 0)),
                  pl.BlockSpec((1, N_HEADS, blk), lambda i, t: (i, 0, t))],
        out_specs=[tok, tok, tok, pl.BlockSpec((1, WIDTH, blk), lambda i, t: (i, 0, t)), tok],
        out_shape=[wide(F32), wide(BF16), wide(BF16), jax.ShapeDtypeStruct((b, WIDTH, s), BF16), wide(BF16)],
        scratch_shapes=[pltpu.VMEM((GDN_HALO + blk, 3 * WIDTH), F32)],
        compiler_params=_params("parallel", "parallel"),
        name="gdn_prepare",
    )(qkv, qkv, conv_w, aux, gc_row)


SCAN_BLOCK = 256


def _gdn_scan_kernel(u_ref, w_ref, qd_ref, qk_ref, kdt_ref, aux_ref, gate_ref, ng_ref, o_ref, state_ref):
    @pl.when(pl.program_id(1) == 0)
    def _():
        state_ref[...] = jnp.zeros_like(state_ref)

    zeros = jnp.zeros((CHUNK, HEAD_DIM), BF16)
    for c in range(SCAN_BLOCK // CHUNK):
        rows = slice(c * CHUNK, (c + 1) * CHUNK)
        pair = slice((c // 2) * GDN_PAIR, (c // 2 + 1) * GDN_PAIR)
        last = (c + 1) * CHUNK - 1
        heads = range(N_HEADS)
        head_cols = [slice(h * HEAD_DIM, (h + 1) * HEAD_DIM) for h in heads]
        states = [state_ref[h] for h in heads]
        r1 = [_dot(jnp.concatenate([w_ref[0, rows, head_cols[h]], qd_ref[0, rows, head_cols[h]]], axis=0),
                   states[h].astype(BF16)) for h in heads]
        r2 = []
        for h in heads:
            v_new = (u_ref[0, rows, head_cols[h]] - r1[h][:CHUNK]).astype(BF16)
            v_pad = jnp.concatenate([v_new, zeros] if c % 2 == 0 else [zeros, v_new], axis=0)
            r2.append(_dot(jnp.concatenate([qk_ref[0, rows, head_cols[h]], kdt_ref[0, head_cols[h], pair]],
                                           axis=0), v_pad))
        for h in heads:
            decay_last = jnp.exp(aux_ref[0, last:last + 1, AUX_GC + h:AUX_GC + h + 1])
            state_ref[h] = states[h] * decay_last + r2[h][CHUNK:]
        for h in heads:
            o = r1[h][CHUNK:] + r2[h][:CHUNK]
            o = o * lax.rsqrt(jnp.mean(o * o, axis=-1, keepdims=True) + RMS_EPS) * ng_ref[...]
            gt = gate_ref[0, rows, head_cols[h]]
            o_ref[0, rows, head_cols[h]] = (o * (gt * _sigmoid(gt))).astype(o_ref.dtype)


def gdn_scan(u, w, qd, qk, kdt, aux, gate, norm_g):
    b, s, _ = u.shape
    tok = pl.BlockSpec((1, SCAN_BLOCK, WIDTH), lambda i, t: (i, t, 0))
    return pl.pallas_call(
        _gdn_scan_kernel,
        grid=(b, s // SCAN_BLOCK),
        in_specs=[tok, tok, tok, tok,
                  pl.BlockSpec((1, WIDTH, SCAN_BLOCK), lambda i, t: (i, 0, t)),
                  pl.BlockSpec((1, SCAN_BLOCK, AUX_LANES), lambda i, t: (i, t, 0)),
                  tok, pl.BlockSpec((1, HEAD_DIM), lambda i, t: (0, 0))],
        out_specs=tok,
        out_shape=jax.ShapeDtypeStruct((b, s, WIDTH), BF16),
        scratch_shapes=[pltpu.VMEM((N_HEADS, HEAD_DIM, HEAD_DIM), F32)],
        compiler_params=_params("parallel", "arbitrary"),
        name="gdn_scan",
    )(u, w, qd, qk, kdt, aux, gate, norm_g)


def gdn_mixer(qkv, conv_w, aux, gc_row, gate, norm_g):
    u, w, qd, kdt, qk = gdn_prepare(qkv, conv_w, aux, gc_row)
    return gdn_scan(u, w, qd, qk, kdt, aux, gate, norm_g)


MERGE_TM = 512
MERGE_TN = 512


def _merge_kernel(xb_ref, ya_ref, yb_ref, yc_ref, wg0, wg1, wg2, bg0, bg1, bg2, wa, wb, wc, o_ref):
    xb = xb_ref[...]

    def branch(y_ref, wp, wg, bg):
        return _sigmoid(_dot(xb, wg[...]) + bg[...]) * _dot(y_ref[...], wp[...])

    merged = branch(ya_ref, wa, wg0, bg0) + branch(yb_ref, wb, wg1, bg1) + branch(yc_ref, wc, wg2, bg2)
    o_ref[...] = merged.astype(o_ref.dtype)


def merge_branches(xb, ya, yb, yc, wg, bg, wp):
    m = xb.shape[0]
    row = lambda width: pl.BlockSpec((MERGE_TM, width), lambda i, j: (i, 0))
    col = lambda depth: pl.BlockSpec((depth, MERGE_TN), lambda i, j: (0, j))
    return pl.pallas_call(
        _merge_kernel,
        grid=(m // MERGE_TM, D_MODEL // MERGE_TN),
        in_specs=[row(D_MODEL), row(WIDTH), row(WIDTH), row(WIDTH),
                  col(D_MODEL), col(D_MODEL), col(D_MODEL), col(1), col(1), col(1),
                  col(WIDTH), col(WIDTH), col(WIDTH)],
        out_specs=pl.BlockSpec((MERGE_TM, MERGE_TN), lambda i, j: (i, j)),
        out_shape=jax.ShapeDtypeStruct((m, D_MODEL), BF16),
        compiler_params=_params("parallel", "arbitrary"),
        name="merge_branches",
    )(xb, ya, yb, yc, *wg, *bg, *wp)


OUT_TM = 256


def _proj_ln_kernel(y_ref, w_ref, x_ref, g_ref, b_ref, o_ref, ob_ref):
    r = DEEPNORM_ALPHA * x_ref[...] + _dot(y_ref[...], w_ref[...])
    out = _layer_norm_rows(r, g_ref[...], b_ref[...])
    o_ref[...] = out
    ob_ref[...] = out.astype(BF16)


def proj_residual_ln(y, w, x, g, b):
    m, k = y.shape
    vec = pl.BlockSpec((1, D_MODEL), lambda i: (0, 0))
    row = pl.BlockSpec((OUT_TM, D_MODEL), lambda i: (i, 0))
    return pl.pallas_call(
        _proj_ln_kernel,
        grid=(m // OUT_TM,),
        in_specs=[pl.BlockSpec((OUT_TM, k), lambda i: (i, 0)),
                  pl.BlockSpec((k, D_MODEL), lambda i: (0, 0)),
                  row, vec, vec],
        out_specs=[row, row],
        out_shape=[jax.ShapeDtypeStruct((m, D_MODEL), F32), jax.ShapeDtypeStruct((m, D_MODEL), BF16)],
        compiler_params=_params("parallel"),
        name="proj_residual_ln",
    )(y, w, x, g, b)


DOWN_TM = 512
DOWN_TK = D_FF_PAD // 4


def _ffn_down_kernel(a_ref, w_ref, x_ref, g_ref, b_ref, o_ref, ob_ref, acc_ref):
    kk = pl.program_id(1)
    last = pl.num_programs(1) - 1

    @pl.when(kk == 0)
    def _():
        acc_ref[...] = DEEPNORM_ALPHA * x_ref[...] + _dot(a_ref[...], w_ref[...])

    @pl.when((kk > 0) & (kk < last))
    def _():
        acc_ref[...] += _dot(a_ref[...], w_ref[...])

    @pl.when(kk == last)
    def _():
        out = _layer_norm_rows(acc_ref[...] + _dot(a_ref[...], w_ref[...]), g_ref[...], b_ref[...])
        o_ref[...] = out
        ob_ref[...] = out.astype(BF16)


def ffn_down_ln(act, w, x, g, b):
    m, k = act.shape
    assert k // DOWN_TK >= 2
    vec = pl.BlockSpec((1, D_MODEL), lambda i, j: (0, 0))
    row = pl.BlockSpec((DOWN_TM, D_MODEL), lambda i, j: (i, 0))
    return pl.pallas_call(
        _ffn_down_kernel,
        grid=(m // DOWN_TM, k // DOWN_TK),
        in_specs=[pl.BlockSpec((DOWN_TM, DOWN_TK), lambda i, j: (i, j)),
                  pl.BlockSpec((DOWN_TK, D_MODEL), lambda i, j: (j, 0)),
                  row, vec, vec],
        out_specs=[row, row],
        out_shape=[jax.ShapeDtypeStruct((m, D_MODEL), F32), jax.ShapeDtypeStruct((m, D_MODEL), BF16)],
        scratch_shapes=[pltpu.VMEM((DOWN_TM, D_MODEL), F32)],
        compiler_params=_params("parallel", "arbitrary"),
        name="ffn_down_ln",
    )(act, w, x, g, b)


FFN_TM = 1024
FFN_TN = 512
FFN_HALO = 16


def _stage_rows_with_halo(x_ref, halo_ref, xs_ref, first):
    halo = halo_ref[...]
    xs_ref[0:FFN_HALO, :] = jnp.where(first, jnp.zeros_like(halo), halo)
    xs_ref[FFN_HALO:, :] = x_ref[...]


def _causal_taps(r, cw, taps):
    tm = r.shape[0] - FFN_HALO
    y = cw[taps - 1:taps] * r[FFN_HALO:FFN_HALO + tm]
    for j in range(taps - 1):
        off = FFN_HALO - (taps - 1) + j
        y = y + cw[j:j + 1] * r[off:off + tm]
    return y


def _ffn_up_kernel(x_ref, halo_ref, wg_ref, wv_ref, cwg_ref, cwv_ref, cbg_ref, cbv_ref, o_ref, xs_ref,
                   *, tiles_per_seq):
    @pl.when(pl.program_id(1) == 0)
    def _():
        _stage_rows_with_halo(x_ref, halo_ref, xs_ref, (pl.program_id(0) % tiles_per_seq) == 0)

    xs = xs_ref[...]
    hg = cbg_ref[...] + _causal_taps(_dot(xs, wg_ref[...]), cwg_ref[...], FFN_CONV)
    hv = cbv_ref[...] + _causal_taps(_dot(xs, wv_ref[...]), cwv_ref[...], FFN_CONV)
    o_ref[...] = (hg * _sigmoid(hg) * hv).astype(o_ref.dtype)


def ffn_up(xb, seq_len, wg, wv, cwg, cwv, cbg, cbv):
    m, k = xb.shape
    n = wg.shape[1]
    assert seq_len % FFN_TM == 0
    halo_blocks = FFN_TM // FFN_HALO
    col = lambda depth: pl.BlockSpec((depth, FFN_TN), lambda i, j: (0, j))
    return pl.pallas_call(
        functools.partial(_ffn_up_kernel, tiles_per_seq=seq_len // FFN_TM),
        grid=(m // FFN_TM, n // FFN_TN),
        in_specs=[pl.BlockSpec((FFN_TM, k), lambda i, j: (i, 0)),
                  pl.BlockSpec((FFN_HALO, k), lambda i, j: (jnp.maximum(i * halo_blocks - 1, 0), 0)),
                  col(k), col(k), col(FFN_CONV), col(FFN_CONV), col(1), col(1)],
        out_specs=pl.BlockSpec((FFN_TM, FFN_TN), lambda i, j: (i, j)),
        out_shape=jax.ShapeDtypeStruct((m, n), BF16),
        scratch_shapes=[pltpu.VMEM((FFN_HALO + FFN_TM, k), BF16)],
        compiler_params=_params("parallel", "arbitrary"),
        name="ffn_up",
    )(xb, xb, wg, wv, cwg, cwv, cbg, cbv)


_FOX_QKV = (0, 3 * WIDTH)
_FOX_F = (_FOX_QKV[1], _FOX_QKV[1] + N_HEADS)
_SGU_UV = (_FOX_F[1], _FOX_F[1] + 2 * WIDTH)
_GDN_QKV = (_SGU_UV[1], _SGU_UV[1] + 3 * WIDTH)
_GDN_A = (_GDN_QKV[1], _GDN_QKV[1] + N_HEADS)
_GDN_B = (_GDN_A[1], _GDN_A[1] + N_HEADS)
_GDN_GATE = (_GDN_B[1], _GDN_B[1] + WIDTH)
_GATES = (_GDN_GATE[1], _GDN_GATE[1] + 3 * D_MODEL)


def _cols(w, span):
    return w[..., span[0]:span[1]]


def _pad_lanes(v, offset):
    return jnp.zeros((1, AUX_LANES), F32).at[0, offset:offset + v.shape[0]].set(v)


def _layer(x, xb, bsz, seq, w_in, b_in, sgu_ln_g, sgu_ln_b, sgu_w, sgu_b, gdn_conv_w, gdn_a_log,
           gdn_dt_bias, gdn_norm_g, w_proj_a, w_proj_b, w_proj_c, w_out, ln1_g, ln1_b,
           ffn_w_up, ffn_conv_w, ffn_conv_b, ffn_w_down, ln2_g, ln2_b):
    tokens = bsz * seq
    wb = lambda span: _cols(w_in, span).astype(BF16)
    bias = lambda span: _cols(b_in, span)[None, :]

    fox_q_t, fox_k, fox_v_t = fox_in_proj(xb, bsz, seq, wb(_FOX_QKV), bias(_FOX_QKV))
    sgu_uv = matmul_bias(xb, wb(_SGU_UV), bias(_SGU_UV), F32, 1024, 1024, "in_sgu")
    gdn_qkv = matmul_bias(xb, wb(_GDN_QKV), bias(_GDN_QKV), F32, 1024, 1024, "in_gdn")
    pad = AUX_LANES - 3 * N_HEADS
    scalar_spans = (_GDN_GATE, _FOX_F, _GDN_A, _GDN_B)
    w_gs = jnp.concatenate([_cols(w_in, sp) for sp in scalar_spans] + [jnp.zeros((D_MODEL, pad), F32)],
                           axis=1).astype(BF16)
    b_gs = jnp.concatenate([_cols(b_in, sp) for sp in scalar_spans] + [jnp.zeros((pad,), F32)])[None, :]
    gate_scalars = matmul_bias(xb, w_gs, b_gs, F32, 1024, WIDTH + AUX_LANES, "in_gate_scalars")
    gate_scalars = gate_scalars.reshape(bsz, seq, WIDTH + AUX_LANES)

    aux, aux_t = aux_scalars(gate_scalars, _pad_lanes(gdn_a_log, AUX_GC), _pad_lanes(gdn_dt_bias, AUX_GC))
    c_row = aux_t[:, AUX_C:AUX_C + N_HEADS, :].reshape(bsz * N_HEADS, seq // FOX_T, FOX_T)
    gc_row = aux_t[:, AUX_GC:AUX_GC + N_HEADS, :]

    y_a = fox_attention(fox_q_t, fox_k.reshape(bsz, seq, WIDTH), fox_v_t, aux, c_row)
    y_b = spatial_gating(sgu_uv.reshape(bsz, seq, 2 * WIDTH), sgu_ln_g[None, :], sgu_ln_b[None, :],
                         sgu_w, sgu_b.T)
    y_c = gdn_mixer(gdn_qkv.reshape(bsz, seq, 3 * WIDTH), gdn_conv_w, aux, gc_row,
                    gate_scalars, gdn_norm_g[None, :])

    g0 = _GATES[0]
    wg = [w_in[:, g0 + i * D_MODEL:g0 + (i + 1) * D_MODEL].astype(BF16) for i in range(3)]
    bg = [b_in[None, g0 + i * D_MODEL:g0 + (i + 1) * D_MODEL] for i in range(3)]
    wp = [w_proj_a.astype(BF16), w_proj_b.astype(BF16), w_proj_c.astype(BF16)]
    merged = merge_branches(xb, y_a.reshape(tokens, WIDTH), y_b.reshape(tokens, WIDTH),
                            y_c.reshape(tokens, WIDTH), wg, bg, wp)
    x, xb = proj_residual_ln(merged, w_out.astype(BF16), x, ln1_g[None, :], ln1_b[None, :])

    ff_pad = D_FF_PAD - D_FF
    pad_cols = lambda t: jnp.pad(t, ((0, 0), (0, ff_pad)))
    act = ffn_up(xb, seq,
                 pad_cols(ffn_w_up[:, :D_FF].astype(BF16)), pad_cols(ffn_w_up[:, D_FF:].astype(BF16)),
                 pad_cols(ffn_conv_w[:, :D_FF]), pad_cols(ffn_conv_w[:, D_FF:]),
                 pad_cols(ffn_conv_b[None, :D_FF]), pad_cols(ffn_conv_b[None, D_FF:]))
    w_down = jnp.pad(ffn_w_down.astype(BF16), ((0, ff_pad), (0, 0)))
    return ffn_down_ln(act, w_down, x, ln2_g[None, :], ln2_b[None, :])


@jax.jit
def kernel(x, w_in, b_in, sgu_ln_g, sgu_ln_b, sgu_w, sgu_b, gdn_conv_w, gdn_a_log, gdn_dt_bias, gdn_norm_g, w_proj_a, w_proj_b, w_proj_c, w_out, ln1_g, ln1_b, ffn_w_up, ffn_conv_w, ffn_conv_b, ffn_w_down, ln2_g, ln2_b):
    bsz, seq, _ = x.shape
    stacked = (w_in, b_in, sgu_ln_g, sgu_ln_b, sgu_w, sgu_b, gdn_conv_w, gdn_a_log, gdn_dt_bias,
               gdn_norm_g, w_proj_a, w_proj_b, w_proj_c, w_out, ln1_g, ln1_b, ffn_w_up, ffn_conv_w,
               ffn_conv_b, ffn_w_down, ln2_g, ln2_b)
    x = x.reshape(bsz * seq, D_MODEL)
    xb = x.astype(BF16)
    for layer in range(w_in.shape[0]):
        x, xb = _layer(x, xb, bsz, seq, *(t[layer] for t in stacked))
    return x.reshape(bsz, seq, D_MODEL)
```

```python
import functools

import jax
import jax.numpy as jnp
from jax import lax
from jax.experimental import pallas as pl
from jax.experimental.pallas import tpu as pltpu

F32 = jnp.float32
BF16 = jnp.bfloat16

D_MODEL = 2048
HEAD_DIM = 128
N_HEADS = 8
WIDTH = N_HEADS * HEAD_DIM
CHUNK = 64
SGU_SPAN = 128
GDN_CONV = 4
D_FF = 5504
D_FF_PAD = 5632
FFN_CONV = 3
DEPTH = 4
DEEPNORM_ALPHA = (2 * DEPTH) ** 0.25
LN_EPS = 1e-5
RMS_EPS = 1e-6

AUX_LANES = 128
AUX_C = 0
AUX_GC = 8
AUX_BETA = 16

VMEM_LIMIT = 56 * 1024 * 1024

NEG_INF = float("-inf")


def _params(*semantics):
    return pltpu.CompilerParams(dimension_semantics=semantics, vmem_limit_bytes=VMEM_LIMIT)


def _dot(a, b):
    return jnp.dot(a, b, preferred_element_type=F32)


def _dot_nt(a, b):
    return lax.dot_general(a, b, (((1,), (1,)), ((), ())), preferred_element_type=F32)


def _split3(x):
    hi = x.astype(BF16)
    r1 = x - hi.astype(F32)
    mid = r1.astype(BF16)
    lo = (r1 - mid.astype(F32)).astype(BF16)
    return hi, mid, lo


def _softplus(z):
    return jnp.maximum(z, 0.0) + jnp.log1p(jnp.exp(-jnp.abs(z)))


def _sigmoid(z):
    return 1.0 / (1.0 + jnp.exp(-z))


def _layer_norm_rows(y, g, b):
    mu = jnp.mean(y, axis=-1, keepdims=True)
    yc = y - mu
    var = jnp.mean(yc * yc, axis=-1, keepdims=True)
    return yc * lax.rsqrt(var + LN_EPS) * g + b


def _mm_bias_kernel(x_ref, w_ref, b_ref, o_ref):
    acc = _dot(x_ref[...], w_ref[...])
    o_ref[...] = (acc + b_ref[...]).astype(o_ref.dtype)


def matmul_bias(xb, w, b, out_dtype, tm, tn, name):
    m, k = xb.shape
    n = w.shape[1]
    return pl.pallas_call(
        _mm_bias_kernel,
        grid=(m // tm, n // tn),
        in_specs=[pl.BlockSpec((tm, k), lambda i, j: (i, 0)),
                  pl.BlockSpec((k, tn), lambda i, j: (0, j)),
                  pl.BlockSpec((1, tn), lambda i, j: (0, j))],
        out_specs=pl.BlockSpec((tm, tn), lambda i, j: (i, j)),
        out_shape=jax.ShapeDtypeStruct((m, n), out_dtype),
        compiler_params=_params("parallel", "arbitrary"),
        name=name,
    )(xb, w, b)


AUX_BLOCK = 256


def _aux_kernel(z_ref, alog_ref, dt_ref, o_ref, ot_ref, carry_ref):
    @pl.when(pl.program_id(1) == 0)
    def _():
        carry_ref[...] = jnp.zeros_like(carry_ref)

    z = z_ref[0]
    lane = lax.broadcasted_iota(jnp.int32, z.shape, 1)
    log_f = -_softplus(-z)
    g = -jnp.exp(alog_ref[...]) * _softplus(z + dt_ref[...])
    beta = _sigmoid(z)

    ri = lax.broadcasted_iota(jnp.int32, (AUX_BLOCK, AUX_BLOCK), 0)
    ci = lax.broadcasted_iota(jnp.int32, (AUX_BLOCK, AUX_BLOCK), 1)
    tri = ri >= ci
    ones_seq = jnp.where(tri, 1.0, 0.0).astype(BF16)
    ones_chunk = jnp.where(tri & ((ri // CHUNK) == (ci // CHUNK)), 1.0, 0.0).astype(BF16)

    def cumsum(ones, val):
        hi, mid, lo = _split3(val)
        return _dot(ones, hi) + _dot(ones, mid) + _dot(ones, lo)

    c = cumsum(ones_seq, log_f) + carry_ref[...]
    carry_ref[...] = c[AUX_BLOCK - 1:AUX_BLOCK, :]
    gc = cumsum(ones_chunk, g)
    out = jnp.where(lane < AUX_GC, c, jnp.where(lane < AUX_BETA, gc, beta))
    o_ref[0] = out
    ot_ref[0] = out.T


def aux_scalars(z, alog_pad, dt_pad):
    b, s, n = z.shape
    last = n // AUX_LANES - 1
    return pl.pallas_call(
        _aux_kernel,
        grid=(b, s // AUX_BLOCK),
        in_specs=[pl.BlockSpec((1, AUX_BLOCK, AUX_LANES), lambda i, j: (i, j, last)),
                  pl.BlockSpec((1, AUX_LANES), lambda i, j: (0, 0)),
                  pl.BlockSpec((1, AUX_LANES), lambda i, j: (0, 0))],
        out_specs=[pl.BlockSpec((1, AUX_BLOCK, AUX_LANES), lambda i, j: (i, j, 0)),
                   pl.BlockSpec((1, AUX_LANES, AUX_BLOCK), lambda i, j: (i, 0, j))],
        out_shape=[jax.ShapeDtypeStruct((b, s, AUX_LANES), F32),
                   jax.ShapeDtypeStruct((b, AUX_LANES, s), F32)],
        scratch_shapes=[pltpu.VMEM((1, AUX_LANES), F32)],
        compiler_params=_params("parallel", "arbitrary"),
        name="aux_scalars",
    )(z, alog_pad, dt_pad)


def _pick_lane(x, idx):
    lane = lax.broadcasted_iota(jnp.int32, x.shape, 1)
    return jnp.sum(jnp.where(lane == idx, x, 0.0), axis=1, keepdims=True)


LOG2E = 1.4426950408889634
FOX_T = 512
FOX_QSCALE = LOG2E * HEAD_DIM ** -0.5
FOX_BIAS_TERMS = 3
FOX_GROUP = 4


def _fox_kernel(qt_ref, k_ref, vt_ref, aux_ref, crow_ref, o_ref, kaug_ref):
    hg = pl.program_id(1)
    qi = pl.program_id(2)
    nk = kaug_ref.shape[1] // FOX_T
    group = range(FOX_GROUP)

    @pl.when(qi == 0)
    def _():
        lane = lax.broadcasted_iota(jnp.int32, (FOX_T, HEAD_DIM), 1)
        for g in group:
            for j in range(nk):
                rows = slice(j * FOX_T, (j + 1) * FOX_T)
                ck = -LOG2E * _pick_lane(aux_ref[0, rows, :], AUX_C + hg * FOX_GROUP + g)
                pieces = [t.astype(F32) for t in _split3(ck)]
                bias = jnp.zeros((FOX_T, HEAD_DIM), F32)
                for i in reversed(range(FOX_BIAS_TERMS)):
                    bias = jnp.where(lane == i, pieces[i], bias)
                kaug_ref[g, rows, 0:HEAD_DIM] = k_ref[0, rows, g * HEAD_DIM:(g + 1) * HEAD_DIM]
                kaug_ref[g, rows, HEAD_DIM:2 * HEAD_DIM] = bias.astype(BF16)

    sub = lax.broadcasted_iota(jnp.int32, (HEAD_DIM, FOX_T), 0)
    ones = jnp.where(sub < FOX_BIAS_TERMS, 1.0, 0.0).astype(BF16)
    q_aug = [jnp.concatenate([qt_ref[0, g, 0], ones], axis=0) for g in group]
    cq = [LOG2E * crow_ref[g, pl.ds(qi, 1), :] for g in group]

    def block(j, carry, diagonal):
        start = pl.multiple_of(j * FOX_T, FOX_T)
        scores = [_dot(kaug_ref[g, pl.ds(start, FOX_T), :], q_aug[g]) for g in group]
        stats, probs = [], []
        for g in group:
            m, l, _ = carry[g]
            s = scores[g]
            if diagonal:
                key = lax.broadcasted_iota(jnp.int32, s.shape, 0)
                qry = lax.broadcasted_iota(jnp.int32, s.shape, 1)
                s = jnp.where(qry >= key, s, NEG_INF)
            m_new = jnp.maximum(m, jnp.max(s, axis=0, keepdims=True) + cq[g])
            alpha = jnp.exp2(m - m_new)
            p = jnp.exp2(s + (cq[g] - m_new))
            stats.append((m_new, alpha * l + jnp.sum(p, axis=0, keepdims=True), alpha))
            probs.append(p.astype(BF16))
        out = []
        for g in group:
            m_new, l, alpha = stats[g]
            acc = alpha * carry[g][2] + _dot(vt_ref[0, g, j], probs[g])
            out.append((m_new, l, acc))
        return tuple(out)

    init = tuple((jnp.full((1, FOX_T), NEG_INF, F32), jnp.zeros((1, FOX_T), F32),
                  jnp.zeros((HEAD_DIM, FOX_T), F32)) for _ in group)
    carry = lax.fori_loop(0, qi, lambda j, c: block(j, c, False), init)
    carry = block(qi, carry, True)
    for g in group:
        _, l, acc = carry[g]
        o_ref[0, :, g * HEAD_DIM:(g + 1) * HEAD_DIM] = (acc / l).T.astype(o_ref.dtype)


FOX_IN_TM = 1024


def _fox_in_kernel(x_ref, w_ref, b_ref, qt_ref, k_ref, vt_ref):
    j = pl.program_id(1)

    def projection():
        return _dot(x_ref[...], w_ref[...]) + b_ref[...]

    def store_transposed(dst_ref, y):
        for t in range(FOX_IN_TM // FOX_T):
            for h in range(N_HEADS):
                tile = y[t * FOX_T:(t + 1) * FOX_T, h * HEAD_DIM:(h + 1) * HEAD_DIM]
                dst_ref[0, h, t] = tile.T.astype(BF16)

    @pl.when(j == 0)
    def _():
        store_transposed(qt_ref, projection() * FOX_QSCALE)

    @pl.when(j == 1)
    def _():
        k_ref[...] = projection().astype(BF16)

    @pl.when(j == 2)
    def _():
        store_transposed(vt_ref, projection())


def fox_in_proj(xb, bsz, seq, w, b):
    m, k = xb.shape
    assert seq % FOX_IN_TM == 0
    per_seq = seq // FOX_IN_TM
    tiles = FOX_IN_TM // FOX_T
    t_spec = pl.BlockSpec((1, N_HEADS, tiles, HEAD_DIM, FOX_T),
                          lambda i, j: (i // per_seq, 0, i % per_seq, 0, 0))
    t_shape = jax.ShapeDtypeStruct((bsz, N_HEADS, seq // FOX_T, HEAD_DIM, FOX_T), BF16)
    return pl.pallas_call(
        _fox_in_kernel,
        grid=(m // FOX_IN_TM, 3),
        in_specs=[pl.BlockSpec((FOX_IN_TM, k), lambda i, j: (i, 0)),
                  pl.BlockSpec((k, WIDTH), lambda i, j: (0, j)),
                  pl.BlockSpec((1, WIDTH), lambda i, j: (0, j))],
        out_specs=[t_spec, pl.BlockSpec((FOX_IN_TM, WIDTH), lambda i, j: (i, 0)), t_spec],
        out_shape=[t_shape, jax.ShapeDtypeStruct((m, WIDTH), BF16), t_shape],
        compiler_params=_params("parallel", "arbitrary"),
        name="in_fox",
    )(xb, w, b)


def fox_attention(q_t, k, v_t, aux, c_row):
    b, s, _ = k.shape
    nt = s // FOX_T
    groups = N_HEADS // FOX_GROUP

    return pl.pallas_call(
        _fox_kernel,
        grid=(b, groups, nt),
        in_specs=[pl.BlockSpec((1, FOX_GROUP, 1, HEAD_DIM, FOX_T), lambda i, h, t: (i, h, t, 0, 0)),
                  pl.BlockSpec((1, s, FOX_GROUP * HEAD_DIM), lambda i, h, t: (i, 0, h)),
                  pl.BlockSpec((1, FOX_GROUP, nt, HEAD_DIM, FOX_T), lambda i, h, t: (i, h, 0, 0, 0)),
                  pl.BlockSpec((1, s, AUX_LANES), lambda i, h, t: (i, 0, 0)),
                  pl.BlockSpec((FOX_GROUP, nt, FOX_T), lambda i, h, t: (i * groups + h, 0, 0))],
        out_specs=pl.BlockSpec((1, FOX_T, FOX_GROUP * HEAD_DIM), lambda i, h, t: (i, t, h)),
        out_shape=jax.ShapeDtypeStruct((b, s, WIDTH), BF16),
        scratch_shapes=[pltpu.VMEM((FOX_GROUP, s, 2 * HEAD_DIM), BF16)],
        compiler_params=_params("parallel", "parallel", "arbitrary"),
        name="fox_attention",
    )(q_t, k, v_t, aux, c_row)


SGU_BLOCK = 4 * SGU_SPAN


def _sgu_kernel(u_ref, v_ref, g_ref, b_ref, w_ref, bst_ref, o_ref):
    ri = lax.broadcasted_iota(jnp.int32, (SGU_SPAN, SGU_SPAN), 0)
    ci = lax.broadcasted_iota(jnp.int32, (SGU_SPAN, SGU_SPAN), 1)
    mask = (ri // CHUNK) >= (ci // CHUNK)
    for g in range(N_HEADS):
        cols = slice(g * HEAD_DIM, (g + 1) * HEAD_DIM)
        wm = jnp.where(mask, w_ref[g], 0.0).astype(BF16)
        spans = [slice(n * SGU_SPAN, (n + 1) * SGU_SPAN) for n in range(SGU_BLOCK // SGU_SPAN)]
        vns = [_layer_norm_rows(v_ref[0, rows, cols], g_ref[:, cols], b_ref[:, cols]).astype(BF16)
               for rows in spans]
        mixed = [_dot(wm, vn) + bst_ref[:, g:g + 1] for vn in vns]
        for rows, mx in zip(spans, mixed):
            o_ref[0, rows, cols] = (u_ref[0, rows, cols] * mx).astype(o_ref.dtype)


def spatial_gating(uv, ln_g, ln_b, w_s, b_s_t):
    b, s, _ = uv.shape
    return pl.pallas_call(
        _sgu_kernel,
        grid=(b, s // SGU_BLOCK),
        in_specs=[pl.BlockSpec((1, SGU_BLOCK, WIDTH), lambda i, j: (i, j, 0)),
                  pl.BlockSpec((1, SGU_BLOCK, WIDTH), lambda i, j: (i, j, 1)),
                  pl.BlockSpec((1, WIDTH), lambda i, j: (0, 0)),
                  pl.BlockSpec((1, WIDTH), lambda i, j: (0, 0)),
                  pl.BlockSpec((N_HEADS, SGU_SPAN, SGU_SPAN), lambda i, j: (0, 0, 0)),
                  pl.BlockSpec((SGU_SPAN, N_HEADS), lambda i, j: (0, 0))],
        out_specs=pl.BlockSpec((1, SGU_BLOCK, WIDTH), lambda i, j: (i, j, 0)),
        out_shape=jax.ShapeDtypeStruct((b, s, WIDTH), BF16),
        compiler_params=_params("parallel", "parallel"),
        name="spatial_gating",
    )(uv, uv, ln_g, ln_b, w_s, b_s_t)


GDN_PAIR = 2 * CHUNK
GDN_HALO = 8
GDN_PREP_PAIRS = 2


def _gdn_prep_kernel(x_ref, halo_ref, cw_ref, aux_ref, gcrow_ref, u_ref, w_ref, qd_ref, kdt_ref, qk_ref,
                     pad_ref):
    first = pl.program_id(1) == 0
    halo = halo_ref[0]
    pad_ref[0:GDN_HALO, :] = jnp.where(first, jnp.zeros_like(halo), halo)
    pad_ref[GDN_HALO:, :] = x_ref[0]

    ri = lax.broadcasted_iota(jnp.int32, (GDN_PAIR, GDN_PAIR), 0)
    ci = lax.broadcasted_iota(jnp.int32, (GDN_PAIR, GDN_PAIR), 1)
    same_chunk = (ri // CHUNK) == (ci // CHUNK)
    causal = same_chunk & (ri >= ci)
    strict = same_chunk & (ri > ci)
    row = lax.broadcasted_iota(jnp.int32, (GDN_PAIR, 1), 0)

    for pair in range(GDN_PREP_PAIRS):
        _gdn_prep_pair(pair, pad_ref, cw_ref, aux_ref, gcrow_ref, causal, strict, row,
                       u_ref, w_ref, qd_ref, kdt_ref, qk_ref)


def _gdn_prep_pair(pair, pad_ref, cw_ref, aux_ref, gcrow_ref, causal, strict, row,
                   u_ref, w_ref, qd_ref, kdt_ref, qk_ref):
    rows = slice(pair * GDN_PAIR, (pair + 1) * GDN_PAIR)
    base = GDN_HALO + pair * GDN_PAIR

    def conv_silu(cols):
        y = cw_ref[GDN_CONV - 1:GDN_CONV, cols] * pad_ref[base:base + GDN_PAIR, cols]
        for j in range(GDN_CONV - 1):
            off = base - (GDN_CONV - 1) + j
            y = y + cw_ref[j:j + 1, cols] * pad_ref[off:off + GDN_PAIR, cols]
        return y * _sigmoid(y)

    def l2n(t):
        return t * lax.rsqrt(jnp.sum(t * t, axis=-1, keepdims=True) + RMS_EPS)

    aux = aux_ref[0, rows, :]
    heads = range(N_HEADS)
    head_cols = {h: slice(h * HEAD_DIM, (h + 1) * HEAD_DIM) for h in heads}
    ys, ps, rhss = {}, {}, {}
    for h in heads:
        cols = head_cols[h]
        q = l2n(conv_silu(slice(h * HEAD_DIM, (h + 1) * HEAD_DIM))) * (HEAD_DIM ** -0.5)
        k = l2n(conv_silu(slice(WIDTH + h * HEAD_DIM, WIDTH + (h + 1) * HEAD_DIM)))
        v = conv_silu(slice(2 * WIDTH + h * HEAD_DIM, 2 * WIDTH + (h + 1) * HEAD_DIM))
        gc = aux[:, AUX_GC + h:AUX_GC + h + 1]
        beta = aux[:, AUX_BETA + h:AUX_BETA + h + 1]
        gc_r = gcrow_ref[0, h:h + 1, rows]
        decay = jnp.exp(jnp.where(causal, gc - gc_r, NEG_INF))
        kb = k * beta
        kbf = k.astype(BF16)
        prod = _dot_nt(jnp.concatenate([kb, q], axis=0).astype(BF16), kbf)
        a = jnp.where(strict, prod[:GDN_PAIR] * decay, 0.0)
        qk_ref[0, rows, cols] = (prod[GDN_PAIR:] * decay).astype(BF16)
        egc = jnp.exp(gc)
        rhss[h] = jnp.concatenate([v * beta, kb * egc], axis=1)
        qd_ref[0, rows, cols] = (q * egc).astype(BF16)
        g_last = jnp.where(row < CHUNK, gc[CHUNK - 1:CHUNK, :], gc[GDN_PAIR - 1:GDN_PAIR, :])
        kdt_ref[0, cols, rows] = (k * jnp.exp(g_last - gc)).T.astype(BF16)
        ab = a.astype(BF16)
        ys[h] = -a
        ps[h] = _dot(ab, ab)

    n = 2
    while True:
        pbs = {h: ps[h].astype(BF16) for h in heads}
        ys = {h: ys[h] + ps[h] + _dot(ys[h].astype(BF16), pbs[h]) for h in heads}
        n *= 2
        if n >= CHUNK:
            break
        ps = {h: _dot(pbs[h], pbs[h]) for h in heads}

    for h in heads:
        sol = rhss[h] + _dot(ys[h].astype(BF16), rhss[h].astype(BF16))
        u_ref[0, rows, head_cols[h]] = sol[:, :HEAD_DIM]
        w_ref[0, rows, head_cols[h]] = sol[:, HEAD_DIM:].astype(BF16)


def gdn_prepare(qkv, conv_w, aux, gc_row):
    b, s, _ = qkv.shape
    blk = GDN_PREP_PAIRS * GDN_PAIR
    halo_blocks = blk // GDN_HALO
    tok = pl.BlockSpec((1, blk, WIDTH), lambda i, t: (i, t, 0))
    wide = lambda dt: jax.ShapeDtypeStruct((b, s, WIDTH), dt)
    return pl.pallas_call(
        _gdn_prep_kernel,
        grid=(b, s // blk),
        in_specs=[pl.BlockSpec((1, blk, 3 * WIDTH), lambda i, t: (i, t, 0)),
                  pl.BlockSpec((1, GDN_HALO, 3 * WIDTH),
                               lambda i, t: (i, jnp.maximum(t * halo_blocks - 1, 0), 0)),
                  pl.BlockSpec((GDN_CONV, 3 * WIDTH), lambda i, t: (0, 0)),
                  pl.BlockSpec((1, blk, AUX_LANES), lambda i, t: (i, t, 0)),
                  pl.BlockSpec((1, N_HEADS, blk), lambda i, t: (i, 0, t))],
        out_specs=[tok, tok, tok, pl.BlockSpec((1, WIDTH, blk), lambda i, t: (i, 0, t)), tok],
        out_shape=[wide(F32), wide(BF16), wide(BF16), jax.ShapeDtypeStruct((b, WIDTH, s), BF16), wide(BF16)],
        scratch_shapes=[pltpu.VMEM((GDN_HALO + blk, 3 * WIDTH), F32)],
        compiler_params=_params("parallel", "parallel"),
        name="gdn_prepare",
    )(qkv, qkv, conv_w, aux, gc_row)


SCAN_BLOCK = 256


def _gdn_scan_kernel(u_ref, w_ref, qd_ref, qk_ref, kdt_ref, aux_ref, gate_ref, ng_ref, o_ref, state_ref):
    @pl.when(pl.program_id(1) == 0)
    def _():
        state_ref[...] = jnp.zeros_like(state_ref)

    zeros = jnp.zeros((CHUNK, HEAD_DIM), BF16)
    for c in range(SCAN_BLOCK // CHUNK):
        rows = slice(c * CHUNK, (c + 1) * CHUNK)
        pair = slice((c // 2) * GDN_PAIR, (c // 2 + 1) * GDN_PAIR)
        last = (c + 1) * CHUNK - 1
        heads = range(N_HEADS)
        head_cols = [slice(h * HEAD_DIM, (h + 1) * HEAD_DIM) for h in heads]
        states = [state_ref[h] for h in heads]
        r1 = [_dot(jnp.concatenate([w_ref[0, rows, head_cols[h]], qd_ref[0, rows, head_cols[h]]], axis=0),
                   states[h].astype(BF16)) for h in heads]
        r2 = []
        for h in heads:
            v_new = (u_ref[0, rows, head_cols[h]] - r1[h][:CHUNK]).astype(BF16)
            v_pad = jnp.concatenate([v_new, zeros] if c % 2 == 0 else [zeros, v_new], axis=0)
            r2.append(_dot(jnp.concatenate([qk_ref[0, rows, head_cols[h]], kdt_ref[0, head_cols[h], pair]],
                                           axis=0), v_pad))
        for h in heads:
            decay_last = jnp.exp(aux_ref[0, last:last + 1, AUX_GC + h:AUX_GC + h + 1])
            state_ref[h] = states[h] * decay_last + r2[h][CHUNK:]
        for h in heads:
            o = r1[h][CHUNK:] + r2[h][:CHUNK]
            o = o * lax.rsqrt(jnp.mean(o * o, axis=-1, keepdims=True) + RMS_EPS) * ng_ref[...]
            gt = gate_ref[0, rows, head_cols[h]]
            o_ref[0, rows, head_cols[h]] = (o * (gt * _sigmoid(gt))).astype(o_ref.dtype)


def gdn_scan(u, w, qd, qk, kdt, aux, gate, norm_g):
    b, s, _ = u.shape
    tok = pl.BlockSpec((1, SCAN_BLOCK, WIDTH), lambda i, t: (i, t, 0))
    return pl.pallas_call(
        _gdn_scan_kernel,
        grid=(b, s // SCAN_BLOCK),
        in_specs=[tok, tok, tok, tok,
                  pl.BlockSpec((1, WIDTH, SCAN_BLOCK), lambda i, t: (i, 0, t)),
                  pl.BlockSpec((1, SCAN_BLOCK, AUX_LANES), lambda i, t: (i, t, 0)),
                  tok, pl.BlockSpec((1, HEAD_DIM), lambda i, t: (0, 0))],
        out_specs=tok,
        out_shape=jax.ShapeDtypeStruct((b, s, WIDTH), BF16),
        scratch_shapes=[pltpu.VMEM((N_HEADS, HEAD_DIM, HEAD_DIM), F32)],
        compiler_params=_params("parallel", "arbitrary"),
        name="gdn_scan",
    )(u, w, qd, qk, kdt, aux, gate, norm_g)


def gdn_mixer(qkv, conv_w, aux, gc_row, gate, norm_g):
    u, w, qd, kdt, qk = gdn_prepare(qkv, conv_w, aux, gc_row)
    return gdn_scan(u, w, qd, qk, kdt, aux, gate, norm_g)


MERGE_TM = 512
MERGE_TN = 512


def _merge_kernel(xb_ref, ya_ref, yb_ref, yc_ref, wg0, wg1, wg2, bg0, bg1, bg2, wa, wb, wc, o_ref):
    xb = xb_ref[...]

    def branch(y_ref, wp, wg, bg):
        return _sigmoid(_dot(xb, wg[...]) + bg[...]) * _dot(y_ref[...], wp[...])

    merged = branch(ya_ref, wa, wg0, bg0) + branch(yb_ref, wb, wg1, bg1) + branch(yc_ref, wc, wg2, bg2)
    o_ref[...] = merged.astype(o_ref.dtype)


def merge_branches(xb, ya, yb, yc, wg, bg, wp):
    m = xb.shape[0]
    row = lambda width: pl.BlockSpec((MERGE_TM, width), lambda i, j: (i, 0))
    col = lambda depth: pl.BlockSpec((depth, MERGE_TN), lambda i, j: (0, j))
    return pl.pallas_call(
        _merge_kernel,
        grid=(m // MERGE_TM, D_MODEL // MERGE_TN),
        in_specs=[row(D_MODEL), row(WIDTH), row(WIDTH), row(WIDTH),
                  col(D_MODEL), col(D_MODEL), col(D_MODEL), col(1), col(1), col(1),
                  col(WIDTH), col(WIDTH), col(WIDTH)],
        out_specs=pl.BlockSpec((MERGE_TM, MERGE_TN), lambda i, j: (i, j)),
        out_shape=jax.ShapeDtypeStruct((m, D_MODEL), BF16),
        compiler_params=_params("parallel", "arbitrary"),
        name="merge_branches",
    )(xb, ya, yb, yc, *wg, *bg, *wp)


OUT_TM = 256


def _proj_ln_kernel(y_ref, w_ref, x_ref, g_ref, b_ref, o_ref, ob_ref):
    r = DEEPNORM_ALPHA * x_ref[...] + _dot(y_ref[...], w_ref[...])
    out = _layer_norm_rows(r, g_ref[...], b_ref[...])
    o_ref[...] = out
    ob_ref[...] = out.astype(BF16)


def proj_residual_ln(y, w, x, g, b):
    m, k = y.shape
    vec = pl.BlockSpec((1, D_MODEL), lambda i: (0, 0))
    row = pl.BlockSpec((OUT_TM, D_MODEL), lambda i: (i, 0))
    return pl.pallas_call(
        _proj_ln_kernel,
        grid=(m // OUT_TM,),
        in_specs=[pl.BlockSpec((OUT_TM, k), lambda i: (i, 0)),
                  pl.BlockSpec((k, D_MODEL), lambda i: (0, 0)),
                  row, vec, vec],
        out_specs=[row, row],
        out_shape=[jax.ShapeDtypeStruct((m, D_MODEL), F32), jax.ShapeDtypeStruct((m, D_MODEL), BF16)],
        compiler_params=_params("parallel"),
        name="proj_residual_ln",
    )(y, w, x, g, b)


DOWN_TM = 256


def _ffn_down_kernel(a_ref, w_hbm_ref, x_ref, g_ref, b_ref, o_ref, ob_ref, w_ref, w_sem):
    @pl.when(pl.program_id(0) == 0)
    def _():
        copy = pltpu.make_async_copy(w_hbm_ref, w_ref, w_sem)
        copy.start()
        copy.wait()

    r = DEEPNORM_ALPHA * x_ref[...] + _dot(a_ref[...], w_ref[...])
    out = _layer_norm_rows(r, g_ref[...], b_ref[...])
    o_ref[...] = out
    ob_ref[...] = out.astype(BF16)


def ffn_down_ln(act, w, x, g, b):
    m, k = act.shape
    vec = pl.BlockSpec((1, D_MODEL), lambda i: (0, 0))
    row = pl.BlockSpec((DOWN_TM, D_MODEL), lambda i: (i, 0))
    return pl.pallas_call(
        _ffn_down_kernel,
        grid=(m // DOWN_TM,),
        in_specs=[pl.BlockSpec((DOWN_TM, k), lambda i: (i, 0)),
                  pl.BlockSpec(memory_space=pl.ANY),
                  row, vec, vec],
        out_specs=[row, row],
        out_shape=[jax.ShapeDtypeStruct((m, D_MODEL), F32), jax.ShapeDtypeStruct((m, D_MODEL), BF16)],
        scratch_shapes=[pltpu.VMEM((k, D_MODEL), BF16), pltpu.SemaphoreType.DMA(())],
        compiler_params=_params("arbitrary"),
        name="ffn_down_ln",
    )(act, w, x, g, b)


FFN_TM = 1024
FFN_TN = 512
FFN_HALO = 16


def _stage_rows_with_halo(x_ref, halo_ref, xs_ref, first):
    halo = halo_ref[...]
    xs_ref[0:FFN_HALO, :] = jnp.where(first, jnp.zeros_like(halo), halo)
    xs_ref[FFN_HALO:, :] = x_ref[...]


def _causal_taps(r, cw, taps):
    tm = r.shape[0] - FFN_HALO
    y = cw[taps - 1:taps] * r[FFN_HALO:FFN_HALO + tm]
    for j in range(taps - 1):
        off = FFN_HALO - (taps - 1) + j
        y = y + cw[j:j + 1] * r[off:off + tm]
    return y


def _ffn_up_kernel(x_ref, halo_ref, wg_ref, wv_ref, cwg_ref, cwv_ref, cbg_ref, cbv_ref, o_ref, xs_ref,
                   *, tiles_per_seq):
    @pl.when(pl.program_id(1) == 0)
    def _():
        _stage_rows_with_halo(x_ref, halo_ref, xs_ref, (pl.program_id(0) % tiles_per_seq) == 0)

    xs = xs_ref[...]
    hg = cbg_ref[...] + _causal_taps(_dot(xs, wg_ref[...]), cwg_ref[...], FFN_CONV)
    hv = cbv_ref[...] + _causal_taps(_dot(xs, wv_ref[...]), cwv_ref[...], FFN_CONV)
    o_ref[...] = (hg * _sigmoid(hg) * hv).astype(o_ref.dtype)


def ffn_up(xb, seq_len, wg, wv, cwg, cwv, cbg, cbv):
    m, k = xb.shape
    n = wg.shape[1]
    assert seq_len % FFN_TM == 0
    halo_blocks = FFN_TM // FFN_HALO
    col = lambda depth: pl.BlockSpec((depth, FFN_TN), lambda i, j: (0, j))
    return pl.pallas_call(
        functools.partial(_ffn_up_kernel, tiles_per_seq=seq_len // FFN_TM),
        grid=(m // FFN_TM, n // FFN_TN),
        in_specs=[pl.BlockSpec((FFN_TM, k), lambda i, j: (i, 0)),
                  pl.BlockSpec((FFN_HALO, k), lambda i, j: (jnp.maximum(i * halo_blocks - 1, 0), 0)),
                  col(k), col(k), col(FFN_CONV), col(FFN_CONV), col(1), col(1)],
        out_specs=pl.BlockSpec((FFN_TM, FFN_TN), lambda i, j: (i, j)),
        out_shape=jax.ShapeDtypeStruct((m, n), BF16),
        scratch_shapes=[pltpu.VMEM((FFN_HALO + FFN_TM, k), BF16)],
        compiler_params=_params("parallel", "arbitrary"),
        name="ffn_up",
    )(xb, xb, wg, wv, cwg, cwv, cbg, cbv)


_FOX_QKV = (0, 3 * WIDTH)
_FOX_F = (_FOX_QKV[1], _FOX_QKV[1] + N_HEADS)
_SGU_UV = (_FOX_F[1], _FOX_F[1] + 2 * WIDTH)
_GDN_QKV = (_SGU_UV[1], _SGU_UV[1] + 3 * WIDTH)
_GDN_A = (_GDN_QKV[1], _GDN_QKV[1] + N_HEADS)
_GDN_B = (_GDN_A[1], _GDN_A[1] + N_HEADS)
_GDN_GATE = (_GDN_B[1], _GDN_B[1] + WIDTH)
_GATES = (_GDN_GATE[1], _GDN_GATE[1] + 3 * D_MODEL)


def _cols(w, span):
    return w[..., span[0]:span[1]]


def _pad_lanes(v, offset):
    return jnp.zeros((1, AUX_LANES), F32).at[0, offset:offset + v.shape[0]].set(v)


def _layer(x, xb, bsz, seq, w_in, b_in, sgu_ln_g, sgu_ln_b, sgu_w, sgu_b, gdn_conv_w, gdn_a_log,
           gdn_dt_bias, gdn_norm_g, w_proj_a, w_proj_b, w_proj_c, w_out, ln1_g, ln1_b,
           ffn_w_up, ffn_conv_w, ffn_conv_b, ffn_w_down, ln2_g, ln2_b):
    tokens = bsz * seq
    wb = lambda span: _cols(w_in, span).astype(BF16)
    bias = lambda span: _cols(b_in, span)[None, :]

    fox_q_t, fox_k, fox_v_t = fox_in_proj(xb, bsz, seq, wb(_FOX_QKV), bias(_FOX_QKV))
    sgu_uv = matmul_bias(xb, wb(_SGU_UV), bias(_SGU_UV), F32, 1024, 1024, "in_sgu")
    gdn_qkv = matmul_bias(xb, wb(_GDN_QKV), bias(_GDN_QKV), F32, 1024, 1024, "in_gdn")
    pad = AUX_LANES - 3 * N_HEADS
    scalar_spans = (_GDN_GATE, _FOX_F, _GDN_A, _GDN_B)
    w_gs = jnp.concatenate([_cols(w_in, sp) for sp in scalar_spans] + [jnp.zeros((D_MODEL, pad), F32)],
                           axis=1).astype(BF16)
    b_gs = jnp.concatenate([_cols(b_in, sp) for sp in scalar_spans] + [jnp.zeros((pad,), F32)])[None, :]
    gate_scalars = matmul_bias(xb, w_gs, b_gs, F32, 1024, WIDTH + AUX_LANES, "in_gate_scalars")
    gate_scalars = gate_scalars.reshape(bsz, seq, WIDTH + AUX_LANES)

    aux, aux_t = aux_scalars(gate_scalars, _pad_lanes(gdn_a_log, AUX_GC), _pad_lanes(gdn_dt_bias, AUX_GC))
    c_row = aux_t[:, AUX_C:AUX_C + N_HEADS, :].reshape(bsz * N_HEADS, seq // FOX_T, FOX_T)
    gc_row = aux_t[:, AUX_GC:AUX_GC + N_HEADS, :]

    y_a = fox_attention(fox_q_t, fox_k.reshape(bsz, seq, WIDTH), fox_v_t, aux, c_row)
    y_b = spatial_gating(sgu_uv.reshape(bsz, seq, 2 * WIDTH), sgu_ln_g[None, :], sgu_ln_b[None, :],
                         sgu_w, sgu_b.T)
    y_c = gdn_mixer(gdn_qkv.reshape(bsz, seq, 3 * WIDTH), gdn_conv_w, aux, gc_row,
                    gate_scalars, gdn_norm_g[None, :])

    g0 = _GATES[0]
    wg = [w_in[:, g0 + i * D_MODEL:g0 + (i + 1) * D_MODEL].astype(BF16) for i in range(3)]
    bg = [b_in[None, g0 + i * D_MODEL:g0 + (i + 1) * D_MODEL] for i in range(3)]
    wp = [w_proj_a.astype(BF16), w_proj_b.astype(BF16), w_proj_c.astype(BF16)]
    merged = merge_branches(xb, y_a.reshape(tokens, WIDTH), y_b.reshape(tokens, WIDTH),
                            y_c.reshape(tokens, WIDTH), wg, bg, wp)
    x, xb = proj_residual_ln(merged, w_out.astype(BF16), x, ln1_g[None, :], ln1_b[None, :])

    ff_pad = D_FF_PAD - D_FF
    pad_cols = lambda t: jnp.pad(t, ((0, 0), (0, ff_pad)))
    act = ffn_up(xb, seq,
                 pad_cols(ffn_w_up[:, :D_FF].astype(BF16)), pad_cols(ffn_w_up[:, D_FF:].astype(BF16)),
                 pad_cols(ffn_conv_w[:, :D_FF]), pad_cols(ffn_conv_w[:, D_FF:]),
                 pad_cols(ffn_conv_b[None, :D_FF]), pad_cols(ffn_conv_b[None, D_FF:]))
    w_down = jnp.pad(ffn_w_down.astype(BF16), ((0, ff_pad), (0, 0)))
    return ffn_down_ln(act, w_down, x, ln2_g[None, :], ln2_b[None, :])


@jax.jit
def kernel(x, w_in, b_in, sgu_ln_g, sgu_ln_b, sgu_w, sgu_b, gdn_conv_w, gdn_a_log, gdn_dt_bias, gdn_norm_g, w_proj_a, w_proj_b, w_proj_c, w_out, ln1_g, ln1_b, ffn_w_up, ffn_conv_w, ffn_conv_b, ffn_w_down, ln2_g, ln2_b):
    bsz, seq, _ = x.shape
    stacked = (w_in, b_in, sgu_ln_g, sgu_ln_b, sgu_w, sgu_b, gdn_conv_w, gdn_a_log, gdn_dt_bias,
               gdn_norm_g, w_proj_a, w_proj_b, w_proj_c, w_out, ln1_g, ln1_b, ffn_w_up, ffn_conv_w,
               ffn_conv_b, ffn_w_down, ln2_g, ln2_b)
    x = x.reshape(bsz * seq, D_MODEL)
    xb = x.astype(BF16)
    for layer in range(w_in.shape[0]):
        x, xb = _layer(x, xb, bsz, seq, *(t[layer] for t in stacked))
    return x.reshape(bsz, seq, D_MODEL)
```

```python
import functools

import jax
import jax.numpy as jnp
from jax import lax
from jax.experimental import pallas as pl
from jax.experimental.pallas import tpu as pltpu

F32 = jnp.float32
BF16 = jnp.bfloat16

D_MODEL = 2048
HEAD_DIM = 128
N_HEADS = 8
WIDTH = N_HEADS * HEAD_DIM
CHUNK = 64
SGU_SPAN = 128
GDN_CONV = 4
D_FF = 5504
D_FF_PAD = 5632
FFN_CONV = 3
DEPTH = 4
DEEPNORM_ALPHA = (2 * DEPTH) ** 0.25
LN_EPS = 1e-5
RMS_EPS = 1e-6

AUX_LANES = 128
AUX_C = 0
AUX_GC = 8
AUX_BETA = 16

VMEM_LIMIT = 56 * 1024 * 1024

NEG_INF = float("-inf")


def _params(*semantics):
    return pltpu.CompilerParams(dimension_semantics=semantics, vmem_limit_bytes=VMEM_LIMIT)


def _dot(a, b):
    return jnp.dot(a, b, preferred_element_type=F32)


def _dot_nt(a, b):
    return lax.dot_general(a, b, (((1,), (1,)), ((), ())), preferred_element_type=F32)


def _split3(x):
    hi = x.astype(BF16)
    r1 = x - hi.astype(F32)
    mid = r1.astype(BF16)
    lo = (r1 - mid.astype(F32)).astype(BF16)
    return hi, mid, lo


def _softplus(z):
    return jnp.maximum(z, 0.0) + jnp.log1p(jnp.exp(-jnp.abs(z)))


def _sigmoid(z):
    return 1.0 / (1.0 + jnp.exp(-z))


def _layer_norm_rows(y, g, b):
    mu = jnp.mean(y, axis=-1, keepdims=True)
    yc = y - mu
    var = jnp.mean(yc * yc, axis=-1, keepdims=True)
    return yc * lax.rsqrt(var + LN_EPS) * g + b


def _mm_bias_kernel(x_ref, w_ref, b_ref, o_ref):
    acc = _dot(x_ref[...], w_ref[...])
    o_ref[...] = (acc + b_ref[...]).astype(o_ref.dtype)


def matmul_bias(xb, w, b, out_dtype, tm, tn, name):
    m, k = xb.shape
    n = w.shape[1]
    return pl.pallas_call(
        _mm_bias_kernel,
        grid=(m // tm, n // tn),
        in_specs=[pl.BlockSpec((tm, k), lambda i, j: (i, 0)),
                  pl.BlockSpec((k, tn), lambda i, j: (0, j)),
                  pl.BlockSpec((1, tn), lambda i, j: (0, j))],
        out_specs=pl.BlockSpec((tm, tn), lambda i, j: (i, j)),
        out_shape=jax.ShapeDtypeStruct((m, n), out_dtype),
        compiler_params=_params("parallel", "arbitrary"),
        name=name,
    )(xb, w, b)


AUX_BLOCK = 256


def _aux_kernel(z_ref, alog_ref, dt_ref, o_ref, ot_ref, carry_ref):
    @pl.when(pl.program_id(1) == 0)
    def _():
        carry_ref[...] = jnp.zeros_like(carry_ref)

    z = z_ref[0]
    lane = lax.broadcasted_iota(jnp.int32, z.shape, 1)
    log_f = -_softplus(-z)
    g = -jnp.exp(alog_ref[...]) * _softplus(z + dt_ref[...])
    beta = _sigmoid(z)

    ri = lax.broadcasted_iota(jnp.int32, (AUX_BLOCK, AUX_BLOCK), 0)
    ci = lax.broadcasted_iota(jnp.int32, (AUX_BLOCK, AUX_BLOCK), 1)
    tri = ri >= ci
    ones_seq = jnp.where(tri, 1.0, 0.0).astype(BF16)
    ones_chunk = jnp.where(tri & ((ri // CHUNK) == (ci // CHUNK)), 1.0, 0.0).astype(BF16)

    def cumsum(ones, val):
        hi, mid, lo = _split3(val)
        return _dot(ones, hi) + _dot(ones, mid) + _dot(ones, lo)

    c = cumsum(ones_seq, log_f) + carry_ref[...]
    carry_ref[...] = c[AUX_BLOCK - 1:AUX_BLOCK, :]
    gc = cumsum(ones_chunk, g)
    out = jnp.where(lane < AUX_GC, c, jnp.where(lane < AUX_BETA, gc, beta))
    o_ref[0] = out
    ot_ref[0] = out.T


def aux_scalars(z, alog_pad, dt_pad):
    b, s, n = z.shape
    last = n // AUX_LANES - 1
    return pl.pallas_call(
        _aux_kernel,
        grid=(b, s // AUX_BLOCK),
        in_specs=[pl.BlockSpec((1, AUX_BLOCK, AUX_LANES), lambda i, j: (i, j, last)),
                  pl.BlockSpec((1, AUX_LANES), lambda i, j: (0, 0)),
                  pl.BlockSpec((1, AUX_LANES), lambda i, j: (0, 0))],
        out_specs=[pl.BlockSpec((1, AUX_BLOCK, AUX_LANES), lambda i, j: (i, j, 0)),
                   pl.BlockSpec((1, AUX_LANES, AUX_BLOCK), lambda i, j: (i, 0, j))],
        out_shape=[jax.ShapeDtypeStruct((b, s, AUX_LANES), F32),
                   jax.ShapeDtypeStruct((b, AUX_LANES, s), F32)],
        scratch_shapes=[pltpu.VMEM((1, AUX_LANES), F32)],
        compiler_params=_params("parallel", "arbitrary"),
        name="aux_scalars",
    )(z, alog_pad, dt_pad)


def _pick_lane(x, idx):
    lane = lax.broadcasted_iota(jnp.int32, x.shape, 1)
    return jnp.sum(jnp.where(lane == idx, x, 0.0), axis=1, keepdims=True)


LOG2E = 1.4426950408889634
FOX_T = 512
FOX_QSCALE = LOG2E * HEAD_DIM ** -0.5
FOX_BIAS_TERMS = 3
FOX_GROUP = 4


def _fox_kernel(qt_ref, k_ref, vt_ref, aux_ref, crow_ref, o_ref, kaug_ref):
    hg = pl.program_id(1)
    qi = pl.program_id(2)
    nk = kaug_ref.shape[1] // FOX_T
    group = range(FOX_GROUP)

    @pl.when(qi == 0)
    def _():
        lane = lax.broadcasted_iota(jnp.int32, (FOX_T, HEAD_DIM), 1)
        for g in group:
            for j in range(nk):
                rows = slice(j * FOX_T, (j + 1) * FOX_T)
                ck = -LOG2E * _pick_lane(aux_ref[0, rows, :], AUX_C + hg * FOX_GROUP + g)
                pieces = [t.astype(F32) for t in _split3(ck)]
                bias = jnp.zeros((FOX_T, HEAD_DIM), F32)
                for i in reversed(range(FOX_BIAS_TERMS)):
                    bias = jnp.where(lane == i, pieces[i], bias)
                kaug_ref[g, rows, 0:HEAD_DIM] = k_ref[0, rows, g * HEAD_DIM:(g + 1) * HEAD_DIM]
                kaug_ref[g, rows, HEAD_DIM:2 * HEAD_DIM] = bias.astype(BF16)

    sub = lax.broadcasted_iota(jnp.int32, (HEAD_DIM, FOX_T), 0)
    ones = jnp.where(sub < FOX_BIAS_TERMS, 1.0, 0.0).astype(BF16)
    q_aug = [jnp.concatenate([qt_ref[0, g, 0], ones], axis=0) for g in group]
    cq = [LOG2E * crow_ref[g, pl.ds(qi, 1), :] for g in group]

    def block(j, carry, diagonal):
        start = pl.multiple_of(j * FOX_T, FOX_T)
        scores = [_dot(kaug_ref[g, pl.ds(start, FOX_T), :], q_aug[g]) for g in group]
        stats, probs = [], []
        for g in group:
            m, l, _ = carry[g]
            s = scores[g]
            if diagonal:
                key = lax.broadcasted_iota(jnp.int32, s.shape, 0)
                qry = lax.broadcasted_iota(jnp.int32, s.shape, 1)
                s = jnp.where(qry >= key, s, NEG_INF)
            m_new = jnp.maximum(m, jnp.max(s, axis=0, keepdims=True) + cq[g])
            alpha = jnp.exp2(m - m_new)
            p = jnp.exp2(s + (cq[g] - m_new))
            stats.append((m_new, alpha * l + jnp.sum(p, axis=0, keepdims=True), alpha))
            probs.append(p.astype(BF16))
        out = []
        for g in group:
            m_new, l, alpha = stats[g]
            acc = alpha * carry[g][2] + _dot(vt_ref[0, g, j], probs[g])
            out.append((m_new, l, acc))
        return tuple(out)

    init = tuple((jnp.full((1, FOX_T), NEG_INF, F32), jnp.zeros((1, FOX_T), F32),
                  jnp.zeros((HEAD_DIM, FOX_T), F32)) for _ in group)
    carry = lax.fori_loop(0, qi, lambda j, c: block(j, c, False), init)
    carry = block(qi, carry, True)
    for g in group:
        _, l, acc = carry[g]
        o_ref[0, :, g * HEAD_DIM:(g + 1) * HEAD_DIM] = (acc / l).T.astype(o_ref.dtype)


FOX_IN_TM = 1024


def _fox_in_kernel(x_ref, w_ref, b_ref, qt_ref, k_ref, vt_ref):
    j = pl.program_id(1)

    def projection():
        return _dot(x_ref[...], w_ref[...]) + b_ref[...]

    def store_transposed(dst_ref, y):
        for t in range(FOX_IN_TM // FOX_T):
            for h in range(N_HEADS):
                tile = y[t * FOX_T:(t + 1) * FOX_T, h * HEAD_DIM:(h + 1) * HEAD_DIM]
                dst_ref[0, h, t] = tile.T.astype(BF16)

    @pl.when(j == 0)
    def _():
        store_transposed(qt_ref, projection() * FOX_QSCALE)

    @pl.when(j == 1)
    def _():
        k_ref[...] = projection().astype(BF16)

    @pl.when(j == 2)
    def _():
        store_transposed(vt_ref, projection())


def fox_in_proj(xb, bsz, seq, w, b):
    m, k = xb.shape
    assert seq % FOX_IN_TM == 0
    per_seq = seq // FOX_IN_TM
    tiles = FOX_IN_TM // FOX_T
    t_spec = pl.BlockSpec((1, N_HEADS, tiles, HEAD_DIM, FOX_T),
                          lambda i, j: (i // per_seq, 0, i % per_seq, 0, 0))
    t_shape = jax.ShapeDtypeStruct((bsz, N_HEADS, seq // FOX_T, HEAD_DIM, FOX_T), BF16)
    return pl.pallas_call(
        _fox_in_kernel,
        grid=(m // FOX_IN_TM, 3),
        in_specs=[pl.BlockSpec((FOX_IN_TM, k), lambda i, j: (i, 0)),
                  pl.BlockSpec((k, WIDTH), lambda i, j: (0, j)),
                  pl.BlockSpec((1, WIDTH), lambda i, j: (0, j))],
        out_specs=[t_spec, pl.BlockSpec((FOX_IN_TM, WIDTH), lambda i, j: (i, 0)), t_spec],
        out_shape=[t_shape, jax.ShapeDtypeStruct((m, WIDTH), BF16), t_shape],
        compiler_params=_params("parallel", "arbitrary"),
        name="in_fox",
    )(xb, w, b)


def fox_attention(q_t, k, v_t, aux, c_row):
    b, s, _ = k.shape
    nt = s // FOX_T
    groups = N_HEADS // FOX_GROUP

    return pl.pallas_call(
        _fox_kernel,
        grid=(b, groups, nt),
        in_specs=[pl.BlockSpec((1, FOX_GROUP, 1, HEAD_DIM, FOX_T), lambda i, h, t: (i, h, t, 0, 0)),
                  pl.BlockSpec((1, s, FOX_GROUP * HEAD_DIM), lambda i, h, t: (i, 0, h)),
                  pl.BlockSpec((1, FOX_GROUP, nt, HEAD_DIM, FOX_T), lambda i, h, t: (i, h, 0, 0, 0)),
                  pl.BlockSpec((1, s, AUX_LANES), lambda i, h, t: (i, 0, 0)),
                  pl.BlockSpec((FOX_GROUP, nt, FOX_T), lambda i, h, t: (i * groups + h, 0, 0))],
        out_specs=pl.BlockSpec((1, FOX_T, FOX_GROUP * HEAD_DIM), lambda i, h, t: (i, t, h)),
        out_shape=jax.ShapeDtypeStruct((b, s, WIDTH), BF16),
        scratch_shapes=[pltpu.VMEM((FOX_GROUP, s, 2 * HEAD_DIM), BF16)],
        compiler_params=_params("parallel", "parallel", "arbitrary"),
        name="fox_attention",
    )(q_t, k, v_t, aux, c_row)


SGU_BLOCK = 4 * SGU_SPAN


def _sgu_kernel(u_ref, v_ref, g_ref, b_ref, w_ref, bst_ref, o_ref):
    ri = lax.broadcasted_iota(jnp.int32, (SGU_SPAN, SGU_SPAN), 0)
    ci = lax.broadcasted_iota(jnp.int32, (SGU_SPAN, SGU_SPAN), 1)
    mask = (ri // CHUNK) >= (ci // CHUNK)
    for g in range(N_HEADS):
        cols = slice(g * HEAD_DIM, (g + 1) * HEAD_DIM)
        wm = jnp.where(mask, w_ref[g], 0.0).astype(BF16)
        spans = [slice(n * SGU_SPAN, (n + 1) * SGU_SPAN) for n in range(SGU_BLOCK // SGU_SPAN)]
        vns = [_layer_norm_rows(v_ref[0, rows, cols], g_ref[:, cols], b_ref[:, cols]).astype(BF16)
               for rows in spans]
        mixed = [_dot(wm, vn) + bst_ref[:, g:g + 1] for vn in vns]
        for rows, mx in zip(spans, mixed):
            o_ref[0, rows, cols] = (u_ref[0, rows, cols] * mx).astype(o_ref.dtype)


def spatial_gating(uv, ln_g, ln_b, w_s, b_s_t):
    b, s, _ = uv.shape
    return pl.pallas_call(
        _sgu_kernel,
        grid=(b, s // SGU_BLOCK),
        in_specs=[pl.BlockSpec((1, SGU_BLOCK, WIDTH), lambda i, j: (i, j, 0)),
                  pl.BlockSpec((1, SGU_BLOCK, WIDTH), lambda i, j: (i, j, 1)),
                  pl.BlockSpec((1, WIDTH), lambda i, j: (0, 0)),
                  pl.BlockSpec((1, WIDTH), lambda i, j: (0, 0)),
                  pl.BlockSpec((N_HEADS, SGU_SPAN, SGU_SPAN), lambda i, j: (0, 0, 0)),
                  pl.BlockSpec((SGU_SPAN, N_HEADS), lambda i, j: (0, 0))],
        out_specs=pl.BlockSpec((1, SGU_BLOCK, WIDTH), lambda i, j: (i, j, 0)),
        out_shape=jax.ShapeDtypeStruct((b, s, WIDTH), BF16),
        compiler_params=_params("parallel", "parallel"),
        name="spatial_gating",
    )(uv, uv, ln_g, ln_b, w_s, b_s_t)


GDN_PAIR = 2 * CHUNK
GDN_HALO = 8
GDN_PREP_PAIRS = 2


def _gdn_prep_kernel(x_ref, halo_ref, cw_ref, aux_ref, gcrow_ref, u_ref, w_ref, qd_ref, kdt_ref, qk_ref,
                     pad_ref):
    first = pl.program_id(1) == 0
    halo = halo_ref[0]
    pad_ref[0:GDN_HALO, :] = jnp.where(first, jnp.zeros_like(halo), halo)
    pad_ref[GDN_HALO:, :] = x_ref[0]

    ri = lax.broadcasted_iota(jnp.int32, (GDN_PAIR, GDN_PAIR), 0)
    ci = lax.broadcasted_iota(jnp.int32, (GDN_PAIR, GDN_PAIR), 1)
    same_chunk = (ri // CHUNK) == (ci // CHUNK)
    causal = same_chunk & (ri >= ci)
    strict = same_chunk & (ri > ci)
    row = lax.broadcasted_iota(jnp.int32, (GDN_PAIR, 1), 0)

    for pair in range(GDN_PREP_PAIRS):
        _gdn_prep_pair(pair, pad_ref, cw_ref, aux_ref, gcrow_ref, causal, strict, row,
                       u_ref, w_ref, qd_ref, kdt_ref, qk_ref)


def _gdn_prep_pair(pair, pad_ref, cw_ref, aux_ref, gcrow_ref, causal, strict, row,
                   u_ref, w_ref, qd_ref, kdt_ref, qk_ref):
    rows = slice(pair * GDN_PAIR, (pair + 1) * GDN_PAIR)
    base = GDN_HALO + pair * GDN_PAIR

    def conv_silu(cols):
        y = cw_ref[GDN_CONV - 1:GDN_CONV, cols] * pad_ref[base:base + GDN_PAIR, cols]
        for j in range(GDN_CONV - 1):
            off = base - (GDN_CONV - 1) + j
            y = y + cw_ref[j:j + 1, cols] * pad_ref[off:off + GDN_PAIR, cols]
        return y * _sigmoid(y)

    def l2n(t):
        return t * lax.rsqrt(jnp.sum(t * t, axis=-1, keepdims=True) + RMS_EPS)

    aux = aux_ref[0, rows, :]
    heads = range(N_HEADS)
    head_cols = {h: slice(h * HEAD_DIM, (h + 1) * HEAD_DIM) for h in heads}
    ys, ps, rhss = {}, {}, {}
    for h in heads:
        cols = head_cols[h]
        q = l2n(conv_silu(slice(h * HEAD_DIM, (h + 1) * HEAD_DIM))) * (HEAD_DIM ** -0.5)
        k = l2n(conv_silu(slice(WIDTH + h * HEAD_DIM, WIDTH + (h + 1) * HEAD_DIM)))
        v = conv_silu(slice(2 * WIDTH + h * HEAD_DIM, 2 * WIDTH + (h + 1) * HEAD_DIM))
        gc = aux[:, AUX_GC + h:AUX_GC + h + 1]
        beta = aux[:, AUX_BETA + h:AUX_BETA + h + 1]
        gc_r = gcrow_ref[0, h:h + 1, rows]
        decay = jnp.exp(jnp.where(causal, gc - gc_r, NEG_INF))
        kb = k * beta
        kbf = k.astype(BF16)
        prod = _dot_nt(jnp.concatenate([kb, q], axis=0).astype(BF16), kbf)
        a = jnp.where(strict, prod[:GDN_PAIR] * decay, 0.0)
        qk_ref[0, rows, cols] = (prod[GDN_PAIR:] * decay).astype(BF16)
        egc = jnp.exp(gc)
        rhss[h] = jnp.concatenate([v * beta, kb * egc], axis=1)
        qd_ref[0, rows, cols] = (q * egc).astype(BF16)
        g_last = jnp.where(row < CHUNK, gc[CHUNK - 1:CHUNK, :], gc[GDN_PAIR - 1:GDN_PAIR, :])
        kdt_ref[0, cols, rows] = (k * jnp.exp(g_last - gc)).T.astype(BF16)
        ab = a.astype(BF16)
        ys[h] = -a
        ps[h] = _dot(ab, ab)

    n = 2
    while True:
        pbs = {h: ps[h].astype(BF16) for h in heads}
        ys = {h: ys[h] + ps[h] + _dot(ys[h].astype(BF16), pbs[h]) for h in heads}
        n *= 2
        if n >= CHUNK:
            break
        ps = {h: _dot(pbs[h], pbs[h]) for h in heads}

    for h in heads:
        sol = rhss[h] + _dot(ys[h].astype(BF16), rhss[h].astype(BF16))
        u_ref[0, rows, head_cols[h]] = sol[:, :HEAD_DIM]
        w_ref[0, rows, head_cols[h]] = sol[:, HEAD_DIM:].astype(BF16)


def gdn_prepare(qkv, conv_w, aux, gc_row):
    b, s, _ = qkv.shape
    blk = GDN_PREP_PAIRS * GDN_PAIR
    halo_blocks = blk // GDN_HALO
    tok = pl.BlockSpec((1, blk, WIDTH), lambda i, t: (i, t, 0))
    wide = lambda dt: jax.ShapeDtypeStruct((b, s, WIDTH), dt)
    return pl.pallas_call(
        _gdn_prep_kernel,
        grid=(b, s // blk),
        in_specs=[pl.BlockSpec((1, blk, 3 * WIDTH), lambda i, t: (i, t, 0)),
                  pl.BlockSpec((1, GDN_HALO, 3 * WIDTH),
                               lambda i, t: (i, jnp.maximum(t * halo_blocks - 1, 0), 0)),
                  pl.BlockSpec((GDN_CONV, 3 * WIDTH), lambda i, t: (0, 0)),
                  pl.BlockSpec((1, blk, AUX_LANES), lambda i, t: (i, t, 0)),
                  pl.BlockSpec((1, N_HEADS, blk), lambda i, t: (i, 0, t))],
        out_specs=[tok, tok, tok, pl.BlockSpec((1, WIDTH, blk), lambda i, t: (i, 0, t)), tok],
        out_shape=[wide(F32), wide(BF16), wide(BF16), jax.ShapeDtypeStruct((b, WIDTH, s), BF16), wide(BF16)],
        scratch_shapes=[pltpu.VMEM((GDN_HALO + blk, 3 * WIDTH), F32)],
        compiler_params=_params("parallel", "parallel"),
        name="gdn_prepare",
    )(qkv, qkv, conv_w, aux, gc_row)


SCAN_BLOCK = 256


def _gdn_scan_kernel(u_ref, w_ref, qd_ref, qk_ref, kdt_ref, aux_ref, gate_ref, ng_ref, o_ref, state_ref):
    @pl.when(pl.program_id(1) == 0)
    def _():
        state_ref[...] = jnp.zeros_like(state_ref)

    zeros = jnp.zeros((CHUNK, HEAD_DIM), BF16)
    for c in range(SCAN_BLOCK // CHUNK):
        rows = slice(c * CHUNK, (c + 1) * CHUNK)
        pair = slice((c // 2) * GDN_PAIR, (c // 2 + 1) * GDN_PAIR)
        last = (c + 1) * CHUNK - 1
        heads = range(N_HEADS)
        head_cols = [slice(h * HEAD_DIM, (h + 1) * HEAD_DIM) for h in heads]
        states = [state_ref[h] for h in heads]
        r1 = [_dot(jnp.concatenate([w_ref[0, rows, head_cols[h]], qd_ref[0, rows, head_cols[h]]], axis=0),
                   states[h].astype(BF16)) for h in heads]
        r2 = []
        for h in heads:
            v_new = (u_ref[0, rows, head_cols[h]] - r1[h][:CHUNK]).astype(BF16)
            v_pad = jnp.concatenate([v_new, zeros] if c % 2 == 0 else [zeros, v_new], axis=0)
            r2.append(_dot(jnp.concatenate([qk_ref[0, rows, head_cols[h]], kdt_ref[0, head_cols[h], pair]],
                                           axis=0), v_pad))
        for h in heads:
            decay_last = jnp.exp(aux_ref[0, last:last + 1, AUX_GC + h:AUX_GC + h + 1])
            state_ref[h] = states[h] * decay_last + r2[h][CHUNK:]
        for h in heads:
            o = r1[h][CHUNK:] + r2[h][:CHUNK]
            o = o * lax.rsqrt(jnp.mean(o * o, axis=-1, keepdims=True) + RMS_EPS) * ng_ref[...]
            gt = gate_ref[0, rows, head_cols[h]]
            o_ref[0, rows, head_cols[h]] = (o * (gt * _sigmoid(gt))).astype(o_ref.dtype)


def gdn_scan(u, w, qd, qk, kdt, aux, gate, norm_g):
    b, s, _ = u.shape
    tok = pl.BlockSpec((1, SCAN_BLOCK, WIDTH), lambda i, t: (i, t, 0))
    return pl.pallas_call(
        _gdn_scan_kernel,
        grid=(b, s // SCAN_BLOCK),
        in_specs=[tok, tok, tok, tok,
                  pl.BlockSpec((1, WIDTH, SCAN_BLOCK), lambda i, t: (i, 0, t)),
                  pl.BlockSpec((1, SCAN_BLOCK, AUX_LANES), lambda i, t: (i, t, 0)),
                  tok, pl.BlockSpec((1, HEAD_DIM), lambda i, t: (0, 0))],
        out_specs=tok,
        out_shape=jax.ShapeDtypeStruct((b, s, WIDTH), BF16),
        scratch_shapes=[pltpu.VMEM((N_HEADS, HEAD_DIM, HEAD_DIM), F32)],
        compiler_params=_params("parallel", "arbitrary"),
        name="gdn_scan",
    )(u, w, qd, qk, kdt, aux, gate, norm_g)


def gdn_mixer(qkv, conv_w, aux, gc_row, gate, norm_g):
    u, w, qd, kdt, qk = gdn_prepare(qkv, conv_w, aux, gc_row)
    return gdn_scan(u, w, qd, qk, kdt, aux, gate, norm_g)


MERGE_TM = 512
MERGE_TN = 512


def _merge_kernel(xb_ref, ya_ref, yb_ref, yc_ref, wg0, wg1, wg2, bg0, bg1, bg2, wa, wb, wc, o_ref):
    xb = xb_ref[...]

    def branch(y_ref, wp, wg, bg):
        return _sigmoid(_dot(xb, wg[...]) + bg[...]) * _dot(y_ref[...], wp[...])

    merged = branch(ya_ref, wa, wg0, bg0) + branch(yb_ref, wb, wg1, bg1) + branch(yc_ref, wc, wg2, bg2)
    o_ref[...] = merged.astype(o_ref.dtype)


def merge_branches(xb, ya, yb, yc, wg, bg, wp):
    m = xb.shape[0]
    row = lambda width: pl.BlockSpec((MERGE_TM, width), lambda i, j: (i, 0))
    col = lambda depth: pl.BlockSpec((depth, MERGE_TN), lambda i, j: (0, j))
    return pl.pallas_call(
        _merge_kernel,
        grid=(m // MERGE_TM, D_MODEL // MERGE_TN),
        in_specs=[row(D_MODEL), row(WIDTH), row(WIDTH), row(WIDTH),
                  col(D_MODEL), col(D_MODEL), col(D_MODEL), col(1), col(1), col(1),
                  col(WIDTH), col(WIDTH), col(WIDTH)],
        out_specs=pl.BlockSpec((MERGE_TM, MERGE_TN), lambda i, j: (i, j)),
        out_shape=jax.ShapeDtypeStruct((m, D_MODEL), BF16),
        compiler_params=_params("parallel", "arbitrary"),
        name="merge_branches",
    )(xb, ya, yb, yc, *wg, *bg, *wp)


OUT_TM = 512
DOWN_TM = 256


def _proj_ln_kernel(a_ref, w_hbm_ref, x_ref, g_ref, b_ref, o_ref, ob_ref, w_ref, w_sem):
    @pl.when(pl.program_id(0) == 0)
    def _():
        copy = pltpu.make_async_copy(w_hbm_ref, w_ref, w_sem)
        copy.start()
        copy.wait()

    r = DEEPNORM_ALPHA * x_ref[...] + _dot(a_ref[...], w_ref[...])
    out = _layer_norm_rows(r, g_ref[...], b_ref[...])
    o_ref[...] = out
    ob_ref[...] = out.astype(BF16)


def proj_residual_ln(act, w, x, g, b, tm, name):
    m, k = act.shape
    vec = pl.BlockSpec((1, D_MODEL), lambda i: (0, 0))
    row = pl.BlockSpec((tm, D_MODEL), lambda i: (i, 0))
    return pl.pallas_call(
        _proj_ln_kernel,
        grid=(m // tm,),
        in_specs=[pl.BlockSpec((tm, k), lambda i: (i, 0)),
                  pl.BlockSpec(memory_space=pl.ANY),
                  row, vec, vec],
        out_specs=[row, row],
        out_shape=[jax.ShapeDtypeStruct((m, D_MODEL), F32), jax.ShapeDtypeStruct((m, D_MODEL), BF16)],
        scratch_shapes=[pltpu.VMEM((k, D_MODEL), BF16), pltpu.SemaphoreType.DMA(())],
        compiler_params=_params("arbitrary"),
        name=name,
    )(act, w, x, g, b)


FFN_TM = 1024
FFN_TN = 512
FFN_HALO = 16


def _stage_rows_with_halo(x_ref, halo_ref, xs_ref, first):
    halo = halo_ref[...]
    xs_ref[0:FFN_HALO, :] = jnp.where(first, jnp.zeros_like(halo), halo)
    xs_ref[FFN_HALO:, :] = x_ref[...]


def _causal_taps(r, cw, taps):
    tm = r.shape[0] - FFN_HALO
    y = cw[taps - 1:taps] * r[FFN_HALO:FFN_HALO + tm]
    for j in range(taps - 1):
        off = FFN_HALO - (taps - 1) + j
        y = y + cw[j:j + 1] * r[off:off + tm]
    return y


def _ffn_up_kernel(x_ref, halo_ref, wg_ref, wv_ref, cwg_ref, cwv_ref, cbg_ref, cbv_ref, o_ref, xs_ref,
                   *, tiles_per_seq):
    @pl.when(pl.program_id(1) == 0)
    def _():
        _stage_rows_with_halo(x_ref, halo_ref, xs_ref, (pl.program_id(0) % tiles_per_seq) == 0)

    xs = xs_ref[...]
    hg = cbg_ref[...] + _causal_taps(_dot(xs, wg_ref[...]), cwg_ref[...], FFN_CONV)
    hv = cbv_ref[...] + _causal_taps(_dot(xs, wv_ref[...]), cwv_ref[...], FFN_CONV)
    o_ref[...] = (hg * _sigmoid(hg) * hv).astype(o_ref.dtype)


def ffn_up(xb, seq_len, wg, wv, cwg, cwv, cbg, cbv):
    m, k = xb.shape
    n = wg.shape[1]
    assert seq_len % FFN_TM == 0
    halo_blocks = FFN_TM // FFN_HALO
    col = lambda depth: pl.BlockSpec((depth, FFN_TN), lambda i, j: (0, j))
    return pl.pallas_call(
        functools.partial(_ffn_up_kernel, tiles_per_seq=seq_len // FFN_TM),
        grid=(m // FFN_TM, n // FFN_TN),
        in_specs=[pl.BlockSpec((FFN_TM, k), lambda i, j: (i, 0)),
                  pl.BlockSpec((FFN_HALO, k), lambda i, j: (jnp.maximum(i * halo_blocks - 1, 0), 0)),
                  col(k), col(k), col(FFN_CONV), col(FFN_CONV), col(1), col(1)],
        out_specs=pl.BlockSpec((FFN_TM, FFN_TN), lambda i, j: (i, j)),
        out_shape=jax.ShapeDtypeStruct((m, n), BF16),
        scratch_shapes=[pltpu.VMEM((FFN_HALO + FFN_TM, k), BF16)],
        compiler_params=_params("parallel", "arbitrary"),
        name="ffn_up",
    )(xb, xb, wg, wv, cwg, cwv, cbg, cbv)


_FOX_QKV = (0, 3 * WIDTH)
_FOX_F = (_FOX_QKV[1], _FOX_QKV[1] + N_HEADS)
_SGU_UV = (_FOX_F[1], _FOX_F[1] + 2 * WIDTH)
_GDN_QKV = (_SGU_UV[1], _SGU_UV[1] + 3 * WIDTH)
_GDN_A = (_GDN_QKV[1], _GDN_QKV[1] + N_HEADS)
_GDN_B = (_GDN_A[1], _GDN_A[1] + N_HEADS)
_GDN_GATE = (_GDN_B[1], _GDN_B[1] + WIDTH)
_GATES = (_GDN_GATE[1], _GDN_GATE[1] + 3 * D_MODEL)


def _cols(w, span):
    return w[..., span[0]:span[1]]


def _pad_lanes(v, offset):
    return jnp.zeros((1, AUX_LANES), F32).at[0, offset:offset + v.shape[0]].set(v)


def _layer(x, xb, bsz, seq, w_in, b_in, sgu_ln_g, sgu_ln_b, sgu_w, sgu_b, gdn_conv_w, gdn_a_log,
           gdn_dt_bias, gdn_norm_g, w_proj_a, w_proj_b, w_proj_c, w_out, ln1_g, ln1_b,
           ffn_w_up, ffn_conv_w, ffn_conv_b, ffn_w_down, ln2_g, ln2_b):
    tokens = bsz * seq
    wb = lambda span: _cols(w_in, span).astype(BF16)
    bias = lambda span: _cols(b_in, span)[None, :]

    fox_q_t, fox_k, fox_v_t = fox_in_proj(xb, bsz, seq, wb(_FOX_QKV), bias(_FOX_QKV))
    sgu_uv = matmul_bias(xb, wb(_SGU_UV), bias(_SGU_UV), F32, 1024, 1024, "in_sgu")
    gdn_qkv = matmul_bias(xb, wb(_GDN_QKV), bias(_GDN_QKV), F32, 1024, 1024, "in_gdn")
    pad = AUX_LANES - 3 * N_HEADS
    scalar_spans = (_GDN_GATE, _FOX_F, _GDN_A, _GDN_B)
    w_gs = jnp.concatenate([_cols(w_in, sp) for sp in scalar_spans] + [jnp.zeros((D_MODEL, pad), F32)],
                           axis=1).astype(BF16)
    b_gs = jnp.concatenate([_cols(b_in, sp) for sp in scalar_spans] + [jnp.zeros((pad,), F32)])[None, :]
    gate_scalars = matmul_bias(xb, w_gs, b_gs, F32, 1024, WIDTH + AUX_LANES, "in_gate_scalars")
    gate_scalars = gate_scalars.reshape(bsz, seq, WIDTH + AUX_LANES)

    aux, aux_t = aux_scalars(gate_scalars, _pad_lanes(gdn_a_log, AUX_GC), _pad_lanes(gdn_dt_bias, AUX_GC))
    c_row = aux_t[:, AUX_C:AUX_C + N_HEADS, :].reshape(bsz * N_HEADS, seq // FOX_T, FOX_T)
    gc_row = aux_t[:, AUX_GC:AUX_GC + N_HEADS, :]

    y_a = fox_attention(fox_q_t, fox_k.reshape(bsz, seq, WIDTH), fox_v_t, aux, c_row)
    y_b = spatial_gating(sgu_uv.reshape(bsz, seq, 2 * WIDTH), sgu_ln_g[None, :], sgu_ln_b[None, :],
                         sgu_w, sgu_b.T)
    y_c = gdn_mixer(gdn_qkv.reshape(bsz, seq, 3 * WIDTH), gdn_conv_w, aux, gc_row,
                    gate_scalars, gdn_norm_g[None, :])

    g0 = _GATES[0]
    wg = [w_in[:, g0 + i * D_MODEL:g0 + (i + 1) * D_MODEL].astype(BF16) for i in range(3)]
    bg = [b_in[None, g0 + i * D_MODEL:g0 + (i + 1) * D_MODEL] for i in range(3)]
    wp = [w_proj_a.astype(BF16), w_proj_b.astype(BF16), w_proj_c.astype(BF16)]
    merged = merge_branches(xb, y_a.reshape(tokens, WIDTH), y_b.reshape(tokens, WIDTH),
                            y_c.reshape(tokens, WIDTH), wg, bg, wp)
    x, xb = proj_residual_ln(merged, w_out.astype(BF16), x, ln1_g[None, :], ln1_b[None, :], OUT_TM, "out_proj_ln")

    ff_pad = D_FF_PAD - D_FF
    pad_cols = lambda t: jnp.pad(t, ((0, 0), (0, ff_pad)))
    act = ffn_up(xb, seq,
                 pad_cols(ffn_w_up[:, :D_FF].astype(BF16)), pad_cols(ffn_w_up[:, D_FF:].astype(BF16)),
                 pad_cols(ffn_conv_w[:, :D_FF]), pad_cols(ffn_conv_w[:, D_FF:]),
                 pad_cols(ffn_conv_b[None, :D_FF]), pad_cols(ffn_conv_b[None, D_FF:]))
    w_down = jnp.pad(ffn_w_down.astype(BF16), ((0, ff_pad), (0, 0)))
    return proj_residual_ln(act, w_down, x, ln2_g[None, :], ln2_b[None, :], DOWN_TM, "ffn_down_ln")


@jax.jit
def kernel(x, w_in, b_in, sgu_ln_g, sgu_ln_b, sgu_w, sgu_b, gdn_conv_w, gdn_a_log, gdn_dt_bias, gdn_norm_g, w_proj_a, w_proj_b, w_proj_c, w_out, ln1_g, ln1_b, ffn_w_up, ffn_conv_w, ffn_conv_b, ffn_w_down, ln2_g, ln2_b):
    bsz, seq, _ = x.shape
    stacked = (w_in, b_in, sgu_ln_g, sgu_ln_b, sgu_w, sgu_b, gdn_conv_w, gdn_a_log, gdn_dt_bias,
               gdn_norm_g, w_proj_a, w_proj_b, w_proj_c, w_out, ln1_g, ln1_b, ffn_w_up, ffn_conv_w,
               ffn_conv_b, ffn_w_down, ln2_g, ln2_b)
    x = x.reshape(bsz * seq, D_MODEL)
    xb = x.astype(BF16)
    for layer in range(w_in.shape[0]):
        x, xb = _layer(x, xb, bsz, seq, *(t[layer] for t in stacked))
    return x.reshape(bsz, seq, D_MODEL)
```
